```python
import math
import jax, jax.numpy as jnp
from jax import lax
import numpy as np

D_MODEL = 1024
BATCH = 8
SEQ = 4096
DEPTH = 1

D_FF = 2816
DA_HEADS = 4
DA_QK = 64
DA_V = 2 * DA_QK
DA_WIDTH = DA_HEADS * DA_V
Q_BLOCK = 128
RET_HEADS = 4
RET_DK = 128
RET_DV = 128
RET_WIDTH = RET_HEADS * RET_DV
RET_CHUNK = 128
SPLIT_SIZES = (
    DA_HEADS * 2 * DA_QK,
    DA_HEADS * 2 * DA_QK,
    DA_WIDTH,
    RET_HEADS * RET_DK,
    RET_HEADS * RET_DK,
    RET_WIDTH,
    RET_WIDTH,
    D_MODEL,
    D_MODEL,
)
D_IN = sum(SPLIT_SIZES)
NORM_EPS = 1e-6
SUBLN_EPS = 1e-5
NEG_INF = -1e30

kernel_name = "hybrid_diffattn_retnet_macaron"


def rmsnorm(x, g, eps=NORM_EPS):
    xf = x.astype(jnp.float32)
    y = xf * lax.rsqrt(jnp.mean(xf * xf, axis=-1, keepdims=True) + eps)
    return (y * g.astype(jnp.float32)).astype(x.dtype)


def swiglu(h, w_gate, w_up, w_down):
    return (jax.nn.silu(h @ w_gate) * (h @ w_up)) @ w_down


def diff_attention(q, k, v, lq1, lk1, lq2, lk2, subln_g, lambda_init):
    B, T, _ = q.shape
    dtype = q.dtype
    q = q.reshape(B, T, 2 * DA_HEADS, DA_QK)
    k = k.reshape(B, T, 2 * DA_HEADS, DA_QK)
    v = v.reshape(B, T, DA_HEADS, DA_V).astype(jnp.float32)
    lam = (jnp.exp(jnp.sum(lq1.astype(jnp.float32) * lk1.astype(jnp.float32)))
           - jnp.exp(jnp.sum(lq2.astype(jnp.float32) * lk2.astype(jnp.float32)))
           + lambda_init)
    scale = DA_QK ** -0.5
    nb = T // Q_BLOCK
    qb = jnp.moveaxis(q.reshape(B, nb, Q_BLOCK, 2 * DA_HEADS, DA_QK), 1, 0)
    kpos = jnp.arange(T)

    def block(args):
        q_blk, i = args
        qpos = i * Q_BLOCK + jnp.arange(Q_BLOCK)
        mask = kpos[None, :] <= qpos[:, None]
        s = jnp.einsum('bqmd,bkmd->bmqk', q_blk, k).astype(jnp.float32) * scale
        s = jnp.where(mask[None, None], s, NEG_INF)
        p = jax.nn.softmax(s, axis=-1).reshape(B, DA_HEADS, 2, Q_BLOCK, T)
        a = p[:, :, 0] - lam * p[:, :, 1]
        return jnp.einsum('bhqk,bkhe->bqhe', a, v)

    o = lax.map(block, (qb, jnp.arange(nb, dtype=jnp.int32)))
    o = jnp.moveaxis(o, 0, 1).reshape(B, T, DA_HEADS, DA_V)
    o = rmsnorm(o, subln_g, SUBLN_EPS).astype(jnp.float32) * (1.0 - lambda_init)
    return o.reshape(B, T, DA_WIDTH).astype(dtype)


def rotate_every_two(x):
    x1 = x[..., ::2]
    x2 = x[..., 1::2]
    return jnp.stack((-x2, x1), axis=-1).reshape(x.shape)


def retention(q, k, v, gate):
    B, T, _ = q.shape
    dtype = q.dtype
    f32 = jnp.float32
    q = q.astype(f32).reshape(B, T, RET_HEADS, RET_DK)
    k = k.astype(f32).reshape(B, T, RET_HEADS, RET_DK) * (RET_DK ** -0.5)
    v = v.astype(f32).reshape(B, T, RET_HEADS, RET_DV)
    angle = 1.0 / (10000.0 ** jnp.linspace(0.0, 1.0, RET_DK // 2, dtype=f32))
    angle = jnp.repeat(angle, 2)
    ang = jnp.arange(T, dtype=f32)[:, None] * angle[None, :]
    sin = jnp.sin(ang)[:, None, :]
    cos = jnp.cos(ang)[:, None, :]
    q = q * cos + rotate_every_two(q) * sin
    k = k * cos + rotate_every_two(k) * sin

    C = RET_CHUNK
    N = T // C
    qc = q.reshape(B, N, C, RET_HEADS, RET_DK)
    kc = k.reshape(B, N, C, RET_HEADS, RET_DK)
    vc = v.reshape(B, N, C, RET_HEADS, RET_DV)
    log_g = jnp.log(1.0 - 2.0 ** (-5.0 - jnp.arange(RET_HEADS, dtype=f32)))
    idx = jnp.arange(C, dtype=f32)
    diff = idx[:, None] - idx[None, :]
    dmask = jnp.where(diff[None] >= 0,
                      jnp.exp(log_g[:, None, None] * jnp.maximum(diff, 0.0)[None]), 0.0)
    s = jnp.einsum('bnqhd,bnkhd->bnhqk', qc, kc) * dmask
    intra = jnp.einsum('bnhqk,bnkhe->bnqhe', s, vc)
    zeta = jnp.exp(log_g[:, None] * (C - 1.0 - idx)[None, :])
    kv = jnp.einsum('bnkhd,bnkhe,hk->nbhde', kc, vc, zeta)
    chunk_decay = jnp.exp(log_g * C)[None, :, None, None]

    def step(state, kv_n):
        return chunk_decay * state + kv_n, state

    _, prev = lax.scan(step, jnp.zeros((B, RET_HEADS, RET_DK, RET_DV), f32), kv)
    xi = jnp.exp(log_g[:, None] * (idx + 1.0)[None, :])
    inter = jnp.einsum('bnqhd,nbhde,hq->bnqhe', qc, prev, xi)
    y = (intra + inter).reshape(B, T, RET_HEADS, RET_DV)
    y = y * lax.rsqrt(jnp.mean(y * y, axis=-1, keepdims=True) + NORM_EPS)
    y = y.reshape(B, T, RET_WIDTH) * jax.nn.silu(gate.astype(f32))
    return y.astype(dtype)


def setup_inputs(seed: int = 0) -> dict:
    key = jax.random.key(seed)
    ks = jax.random.split(key, 32)
    L, D, F = DEPTH, D_MODEL, D_FF
    nrm = lambda k, shape, fan: jax.random.normal(k, shape, jnp.float32) * fan ** -0.5
    gain = lambda k, n: 1.0 + 0.02 * jax.random.normal(k, (L, n), jnp.float32)
    return {
        "x": jax.random.normal(ks[0], (BATCH, SEQ, D), jnp.float32),
        "ffn1_pre_g": gain(ks[1], D),
        "ffn1_w_gate": nrm(ks[2], (L, D, F), D),
        "ffn1_w_up": nrm(ks[3], (L, D, F), D),
        "ffn1_w_down": nrm(ks[4], (L, F, D), F),
        "ffn1_post_g": gain(ks[5], D),
        "mix_pre_g": gain(ks[6], D),
        "w_in": nrm(ks[7], (L, D, D_IN), D),
        "lambda_q1": 0.1 * jax.random.normal(ks[8], (L, DA_QK), jnp.float32),
        "lambda_k1": 0.1 * jax.random.normal(ks[9], (L, DA_QK), jnp.float32),
        "lambda_q2": 0.1 * jax.random.normal(ks[10], (L, DA_QK), jnp.float32),
        "lambda_k2": 0.1 * jax.random.normal(ks[11], (L, DA_QK), jnp.float32),
        "diff_subln_g": gain(ks[12], DA_V),
        "w_attn_proj": nrm(ks[13], (L, DA_WIDTH, D), DA_WIDTH),
        "w_ret_proj": nrm(ks[14], (L, RET_WIDTH, D), RET_WIDTH),
        "w_out": nrm(ks[15], (L, D, D), D),
        "mix_post_g": gain(ks[16], D),
        "ffn2_pre_g": gain(ks[17], D),
        "ffn2_w_gate": nrm(ks[18], (L, D, F), D),
        "ffn2_w_up": nrm(ks[19], (L, D, F), D),
        "ffn2_w_down": nrm(ks[20], (L, F, D), F),
        "ffn2_post_g": gain(ks[21], D),
    }


def reference(x, ffn1_pre_g, ffn1_w_gate, ffn1_w_up, ffn1_w_down, ffn1_post_g,
              mix_pre_g, w_in, lambda_q1, lambda_k1, lambda_q2, lambda_k2,
              diff_subln_g, w_attn_proj, w_ret_proj, w_out, mix_post_g,
              ffn2_pre_g, ffn2_w_gate, ffn2_w_up, ffn2_w_down, ffn2_post_g):
    cuts = list(np.cumsum(SPLIT_SIZES)[:-1])
    for l in range(DEPTH):
        lambda_init = 0.8 - 0.6 * math.exp(-0.3 * l)
        h = rmsnorm(x, ffn1_pre_g[l])
        x = x + 0.5 * rmsnorm(swiglu(h, ffn1_w_gate[l], ffn1_w_up[l], ffn1_w_down[l]), ffn1_post_g[l])
        h = rmsnorm(x, mix_pre_g[l])
        z = h @ w_in[l]
        dq, dk, dv, rq, rk, rv, rg, ga, gr = jnp.split(z, cuts, axis=-1)
        y_a = diff_attention(dq, dk, dv, lambda_q1[l], lambda_k1[l], lambda_q2[l],
                             lambda_k2[l], diff_subln_g[l], lambda_init) @ w_attn_proj[l]
        y_r = retention(rq, rk, rv, rg) @ w_ret_proj[l]
        merged = jax.nn.sigmoid(ga) * y_a + jax.nn.sigmoid(gr) * y_r
        x = x + rmsnorm(merged @ w_out[l], mix_post_g[l])
        h = rmsnorm(x, ffn2_pre_g[l])
        x = x + 0.5 * rmsnorm(swiglu(h, ffn2_w_gate[l], ffn2_w_up[l], ffn2_w_down[l]), ffn2_post_g[l])
    return x
```

```python
import functools
import math

import jax
import jax.numpy as jnp
from jax import lax
from jax.experimental import pallas as pl
from jax.experimental.pallas import tpu as pltpu

F32 = jnp.float32
BF16 = jnp.bfloat16

DA_HEADS = 4
DA_QK = 64
HEAD_W = 128
RET_HEADS = 4
NORM_EPS = 1e-6
SUBLN_EPS = 1e-5
NEG_INF = -1e30

VMEM_LIMIT_BYTES = 56 * 1024 * 1024

FFN_TOKENS = 512
ATTN_BLOCK = 256
RET_CHUNK = 256


def _params(*sem):
    return pltpu.CompilerParams(dimension_semantics=sem, vmem_limit_bytes=VMEM_LIMIT_BYTES)


def _resident(shape):
    nd = len(shape)
    return pl.BlockSpec(shape, lambda *_: (0,) * nd, pipeline_mode=pl.Buffered(1))


def _rmsnorm(x, g, eps):
    return x * lax.rsqrt(jnp.mean(x * x, axis=-1, keepdims=True) + eps) * g


def _ffn_kernel(x_ref, pre_g_ref, wg_ref, wu_ref, wd_ref, post_g_ref, o_ref):
    x = x_ref[...]
    h = _rmsnorm(x, pre_g_ref[...], NORM_EPS).astype(BF16)
    g = jnp.dot(h, wg_ref[...], preferred_element_type=F32)
    u = jnp.dot(h, wu_ref[...], preferred_element_type=F32)
    a = (g * jax.nn.sigmoid(g) * u).astype(BF16)
    y = jnp.dot(a, wd_ref[...], preferred_element_type=F32)
    o_ref[...] = x + 0.5 * _rmsnorm(y, post_g_ref[...], NORM_EPS)


def _ffn(x2d, pre_g, wg, wu, wd, post_g):
    n, d = x2d.shape
    f = wg.shape[1]
    tm = min(FFN_TOKENS, n)
    row = pl.BlockSpec((tm, d), lambda i: (i, 0))
    return pl.pallas_call(
        _ffn_kernel,
        grid=(n // tm,),
        in_specs=[row, _resident((1, d)), _resident((d, f)), _resident((d, f)),
                  _resident((f, d)), _resident((1, d))],
        out_specs=row,
        out_shape=jax.ShapeDtypeStruct((n, d), F32),
        compiler_params=_params("parallel"),
        name="ffn",
    )(x2d, pre_g, wg, wu, wd, post_g)


def _in_proj_kernel(x_ref, g_ref, wqkv_ref, wrg_ref, wga_ref, wgr_ref,
                    qkv_ref, rg_ref, ga_ref, gr_ref):
    h = _rmsnorm(x_ref[...], g_ref[...], NORM_EPS).astype(BF16)
    qkv_ref[...] = jnp.dot(h, wqkv_ref[...], preferred_element_type=F32).astype(BF16)
    rg_ref[...] = jnp.dot(h, wrg_ref[...], preferred_element_type=F32)
    ga_ref[...] = jnp.dot(h, wga_ref[...], preferred_element_type=F32)
    gr_ref[...] = jnp.dot(h, wgr_ref[...], preferred_element_type=F32)


def _in_proj(x2d, g, wqkv, wrg, wga, wgr):
    n, d = x2d.shape
    tm = min(FFN_TOKENS, n)
    ws = (wqkv, wrg, wga, wgr)

    def row(w):
        return pl.BlockSpec((tm, w), lambda i: (i, 0))

    return pl.pallas_call(
        _in_proj_kernel,
        grid=(n // tm,),
        in_specs=[row(d), _resident((1, d))] + [_resident(w.shape) for w in ws],
        out_specs=[row(w.shape[1]) for w in ws],
        out_shape=[jax.ShapeDtypeStruct((n, wqkv.shape[1]), BF16)]
        + [jax.ShapeDtypeStruct((n, w.shape[1]), F32) for w in ws[1:]],
        compiler_params=_params("parallel"),
        name="in_proj",
    )(x2d, g, *ws)


def _diff_attn_kernel(q_ref, k_ref, v_ref, lq1_ref, lk1_ref, lq2_ref, lk2_ref, g_ref,
                      o_ref, m_ref, l_ref, acc_ref, *, blk, lambda_init):
    i = pl.program_id(2)
    lane = lax.broadcasted_iota(jnp.int32, (blk, HEAD_W), 1)
    q = q_ref[...] * (DA_QK ** -0.5)
    zero = jnp.zeros_like(q)
    qq = jnp.concatenate([jnp.where(lane < DA_QK, q, zero),
                          jnp.where(lane >= DA_QK, q, zero)], axis=0)

    m_ref[...] = jnp.full(m_ref.shape, NEG_INF, F32)
    l_ref[...] = jnp.zeros(l_ref.shape, F32)
    acc_ref[...] = jnp.zeros(acc_ref.shape, F32)

    def step(j, masked):
        start = pl.multiple_of(j * blk, blk)
        k = k_ref[pl.ds(start, blk), :]
        v = v_ref[pl.ds(start, blk), :]
        s = lax.dot_general(qq, k, (((1,), (1,)), ((), ())), preferred_element_type=F32)
        if masked:
            row = lax.broadcasted_iota(jnp.int32, (2 * blk, blk), 0)
            col = lax.broadcasted_iota(jnp.int32, (2 * blk, blk), 1)
            row = jnp.where(row >= blk, row - blk, row)
            s = jnp.where(col <= row, s, NEG_INF)
        m_old = m_ref[...]
        m_new = jnp.maximum(m_old, jnp.max(s, axis=-1, keepdims=True))
        alpha = jnp.exp(m_old - m_new)
        p = jnp.exp(s - m_new)
        l_ref[...] = alpha * l_ref[...] + jnp.sum(p, axis=-1, keepdims=True)
        acc_ref[...] = alpha * acc_ref[...] + jnp.dot(p.astype(BF16), v, preferred_element_type=F32)
        m_ref[...] = m_new

    def body(j, carry):
        step(j, False)
        return carry

    lax.fori_loop(0, i, body, 0)
    step(i, True)

    lam = (jnp.exp(jnp.sum(lq1_ref[...] * lk1_ref[...], axis=-1, keepdims=True))
           - jnp.exp(jnp.sum(lq2_ref[...] * lk2_ref[...], axis=-1, keepdims=True))
           + lambda_init)
    o = acc_ref[...] / l_ref[...]
    o = o[:blk] - lam * o[blk:]
    o = _rmsnorm(o, g_ref[...], SUBLN_EPS) * (1.0 - lambda_init)
    o_ref[...] = o.astype(o_ref.dtype)


def _diff_attn(qkv, lq1, lk1, lq2, lk2, subln_g, lambda_init, q_col, k_col, v_col):
    b, t, _ = qkv.shape
    blk = min(ATTN_BLOCK, t)
    small = _resident((1, DA_QK))
    return pl.pallas_call(
        functools.partial(_diff_attn_kernel, blk=blk, lambda_init=lambda_init),
        grid=(b, DA_HEADS, t // blk),
        in_specs=[
            pl.BlockSpec((None, blk, HEAD_W), lambda bi, h, i: (bi, i, q_col + h)),
            pl.BlockSpec((None, t, HEAD_W), lambda bi, h, i: (bi, 0, k_col + h)),
            pl.BlockSpec((None, t, HEAD_W), lambda bi, h, i: (bi, 0, v_col + h)),
            small, small, small, small, _resident((1, HEAD_W)),
        ],
        out_specs=pl.BlockSpec((None, blk, HEAD_W), lambda bi, h, i: (bi, i, h)),
        out_shape=jax.ShapeDtypeStruct((b, t, DA_HEADS * HEAD_W), BF16),
        scratch_shapes=[pltpu.VMEM((2 * blk, 1), F32), pltpu.VMEM((2 * blk, 1), F32),
                        pltpu.VMEM((2 * blk, HEAD_W), F32)],
        compiler_params=_params("parallel", "parallel", "arbitrary"),
        name="diff_attn",
    )(qkv, qkv, qkv, lq1, lk1, lq2, lk2, subln_g)


def _rotate_every_two(x):
    lane = lax.broadcasted_iota(jnp.int32, x.shape, 1)
    nxt = pltpu.roll(x, HEAD_W - 1, 1)
    prv = pltpu.roll(x, 1, 1)
    return jnp.where(lane % 2 == 0, -nxt, prv)


def _retention_kernel(q_ref, k_ref, v_ref, gate_ref, sin_ref, cos_ref, logg_ref,
                      o_ref, state_ref, *, chunk):
    c = pl.program_id(2)

    @pl.when(c == 0)
    def _():
        state_ref[...] = jnp.zeros(state_ref.shape, F32)

    sin = sin_ref[...]
    cos = cos_ref[...]
    q = q_ref[...].astype(F32)
    k = k_ref[...].astype(F32) * (HEAD_W ** -0.5)
    q = q * cos + _rotate_every_two(q) * sin
    k = k * cos + _rotate_every_two(k) * sin
    v = v_ref[...]

    log_g = logg_ref[...]
    idx = lax.broadcasted_iota(jnp.int32, (chunk, HEAD_W), 0).astype(F32)
    xi = jnp.exp(log_g * (idx + 1.0))
    zeta = jnp.exp(log_g * (chunk - 1.0 - idx))
    qi = lax.broadcasted_iota(jnp.int32, (chunk, chunk), 0)
    ki = lax.broadcasted_iota(jnp.int32, (chunk, chunk), 1)
    diff = (qi - ki).astype(F32)
    dmask = jnp.where(diff >= 0, jnp.exp(log_g[:, :1] * jnp.maximum(diff, 0.0)), 0.0)

    qb = q.astype(BF16)
    s = lax.dot_general(qb, k.astype(BF16), (((1,), (1,)), ((), ())),
                        preferred_element_type=F32) * dmask
    intra = jnp.dot(s.astype(BF16), v, preferred_element_type=F32)
    state = state_ref[...]
    inter = jnp.dot((q * xi).astype(BF16), state.astype(BF16), preferred_element_type=F32)
    kv = lax.dot_general((k * zeta).astype(BF16), v, (((0,), (0,)), ((), ())),
                         preferred_element_type=F32)
    state_ref[...] = jnp.exp(log_g * float(chunk)) * state + kv

    y = intra + inter
    y = y * lax.rsqrt(jnp.mean(y * y, axis=-1, keepdims=True) + NORM_EPS)
    gate = gate_ref[...]
    o_ref[...] = (y * (gate * jax.nn.sigmoid(gate))).astype(o_ref.dtype)


def _retention(qkv, gate, sin, cos, log_g, q_col, k_col, v_col):
    b, t, _ = qkv.shape
    chunk = min(RET_CHUNK, t)

    def slab(col):
        return pl.BlockSpec((None, chunk, HEAD_W), lambda bi, h, c: (bi, c, col + h))

    table = pl.BlockSpec((chunk, HEAD_W), lambda bi, h, c: (c, 0))
    return pl.pallas_call(
        functools.partial(_retention_kernel, chunk=chunk),
        grid=(b, RET_HEADS, t // chunk),
        in_specs=[slab(q_col), slab(k_col), slab(v_col), slab(0), table, table,
                  pl.BlockSpec((None, 1, HEAD_W), lambda bi, h, c: (h, 0, 0))],
        out_specs=slab(0),
        out_shape=jax.ShapeDtypeStruct((b, t, RET_HEADS * HEAD_W), BF16),
        scratch_shapes=[pltpu.VMEM((HEAD_W, HEAD_W), F32)],
        compiler_params=_params("parallel", "parallel", "arbitrary"),
        name="retention",
    )(qkv, qkv, qkv, gate, sin, cos, log_g)


def _merge_kernel(x_ref, ya_ref, yr_ref, ga_ref, gr_ref, wa_ref, wr_ref, wo_ref, g_ref, o_ref):
    ya = jnp.dot(ya_ref[...], wa_ref[...], preferred_element_type=F32)
    yr = jnp.dot(yr_ref[...], wr_ref[...], preferred_element_type=F32)
    merged = jax.nn.sigmoid(ga_ref[...]) * ya + jax.nn.sigmoid(gr_ref[...]) * yr
    y = jnp.dot(merged.astype(BF16), wo_ref[...], preferred_element_type=F32)
    o_ref[...] = x_ref[...] + _rmsnorm(y, g_ref[...], NORM_EPS)


def _merge(x2d, ya, yr, ga, gr, wa, wr, wo, post_g):
    n, d = x2d.shape
    tm = min(FFN_TOKENS, n)

    def row(w):
        return pl.BlockSpec((tm, w), lambda i: (i, 0))

    return pl.pallas_call(
        _merge_kernel,
        grid=(n // tm,),
        in_specs=[row(d), row(ya.shape[1]), row(yr.shape[1]), row(d), row(d),
                  _resident(wa.shape), _resident(wr.shape), _resident(wo.shape), _resident((1, d))],
        out_specs=row(d),
        out_shape=jax.ShapeDtypeStruct((n, d), F32),
        compiler_params=_params("parallel"),
        name="merge",
    )(x2d, ya, yr, ga, gr, wa, wr, wo, post_g)


def _retention_tables(t):
    angle = 1.0 / (10000.0 ** jnp.linspace(0.0, 1.0, HEAD_W // 2, dtype=F32))
    angle = jnp.repeat(angle, 2)
    ang = jnp.arange(t, dtype=F32)[:, None] * angle[None, :]
    log_g = jnp.log(1.0 - 2.0 ** (-5.0 - jnp.arange(RET_HEADS, dtype=F32)))
    log_g = jnp.broadcast_to(log_g[:, None, None], (RET_HEADS, 1, HEAD_W))
    return jnp.sin(ang), jnp.cos(ang), log_g


def kernel(x, ffn1_pre_g, ffn1_w_gate, ffn1_w_up, ffn1_w_down, ffn1_post_g, mix_pre_g, w_in, lambda_q1, lambda_k1, lambda_q2, lambda_k2, diff_subln_g, w_attn_proj, w_ret_proj, w_out, mix_post_g, ffn2_pre_g, ffn2_w_gate, ffn2_w_up, ffn2_w_down, ffn2_post_g):
    b, t, d = x.shape
    depth = w_in.shape[0]
    n = b * t
    da_w = DA_HEADS * HEAD_W
    ret_w = RET_HEADS * HEAD_W
    qkv_w = 3 * da_w + 3 * ret_w
    sin, cos, log_g = _retention_tables(t)
    bf = lambda w: w.astype(BF16)

    x2d = x.reshape(n, d)
    for l in range(depth):
        lambda_init = 0.8 - 0.6 * math.exp(-0.3 * l)
        x2d = _ffn(x2d, ffn1_pre_g[l][None], bf(ffn1_w_gate[l]), bf(ffn1_w_up[l]),
                   bf(ffn1_w_down[l]), ffn1_post_g[l][None])

        w = bf(w_in[l])
        qkv, rg, ga, gr = _in_proj(
            x2d, mix_pre_g[l][None], w[:, :qkv_w], w[:, qkv_w:qkv_w + ret_w],
            w[:, qkv_w + ret_w:qkv_w + ret_w + d], w[:, qkv_w + ret_w + d:])
        qkv = qkv.reshape(b, t, qkv_w)
        ya = _diff_attn(qkv, lambda_q1[l][None], lambda_k1[l][None], lambda_q2[l][None],
                        lambda_k2[l][None], diff_subln_g[l][None], lambda_init,
                        q_col=0, k_col=DA_HEADS, v_col=2 * DA_HEADS)
        yr = _retention(qkv, rg.reshape(b, t, ret_w), sin, cos, log_g,
                        q_col=3 * DA_HEADS, k_col=3 * DA_HEADS + RET_HEADS,
                        v_col=3 * DA_HEADS + 2 * RET_HEADS)
        x2d = _merge(x2d, ya.reshape(n, da_w), yr.reshape(n, ret_w), ga, gr,
                     bf(w_attn_proj[l]), bf(w_ret_proj[l]), bf(w_out[l]), mix_post_g[l][None])

        x2d = _ffn(x2d, ffn2_pre_g[l][None], bf(ffn2_w_gate[l]), bf(ffn2_w_up[l]),
                   bf(ffn2_w_down[l]), ffn2_post_g[l][None])
    return x2d.reshape(b, t, d)
```

```python
import functools
import math

import jax
import jax.numpy as jnp
from jax import lax
from jax.experimental import pallas as pl
from jax.experimental.pallas import tpu as pltpu

F32 = jnp.float32
BF16 = jnp.bfloat16

DA_HEADS = 4
DA_QK = 64
HEAD_W = 128
RET_HEADS = 4
NORM_EPS = 1e-6
SUBLN_EPS = 1e-5
NEG_INF = -1e30
Q_SCALE = DA_QK ** -0.5 * math.log2(math.e)
ONES_ROWS = 16

VMEM_LIMIT_BYTES = 56 * 1024 * 1024

FFN_TOKENS = 512
ATTN_BLOCK = 256
RET_CHUNK = 256


def _params(*sem):
    return pltpu.CompilerParams(dimension_semantics=sem, vmem_limit_bytes=VMEM_LIMIT_BYTES)


def _resident(shape):
    nd = len(shape)
    return pl.BlockSpec(shape, lambda *_: (0,) * nd, pipeline_mode=pl.Buffered(1))


def _rmsnorm(x, g, eps):
    return x * lax.rsqrt(jnp.mean(x * x, axis=-1, keepdims=True) + eps) * g


def _ffn_kernel(x_ref, pre_g_ref, wg_ref, wu_ref, wd_ref, post_g_ref, o_ref):
    x = x_ref[...]
    h = _rmsnorm(x, pre_g_ref[...], NORM_EPS).astype(BF16)
    g = jnp.dot(h, wg_ref[...], preferred_element_type=F32)
    u = jnp.dot(h, wu_ref[...], preferred_element_type=F32)
    a = (g * jax.nn.sigmoid(g) * u).astype(BF16)
    y = jnp.dot(a, wd_ref[...], preferred_element_type=F32)
    o_ref[...] = x + 0.5 * _rmsnorm(y, post_g_ref[...], NORM_EPS)


def _ffn(x2d, pre_g, wg, wu, wd, post_g):
    n, d = x2d.shape
    f = wg.shape[1]
    tm = min(FFN_TOKENS, n)
    row = pl.BlockSpec((tm, d), lambda i: (i, 0))
    return pl.pallas_call(
        _ffn_kernel,
        grid=(n // tm,),
        in_specs=[row, _resident((1, d)), _resident((d, f)), _resident((d, f)),
                  _resident((f, d)), _resident((1, d))],
        out_specs=row,
        out_shape=jax.ShapeDtypeStruct((n, d), F32),
        compiler_params=_params("parallel"),
        name="ffn",
    )(x2d, pre_g, wg, wu, wd, post_g)


def _in_proj_kernel(x_ref, g_ref, wqkv_ref, wrg_ref, wga_ref, wgr_ref,
                    qkv_ref, rg_ref, ga_ref, gr_ref, *, q_cols):
    h = _rmsnorm(x_ref[...], g_ref[...], NORM_EPS).astype(BF16)
    z = jnp.dot(h, wqkv_ref[...], preferred_element_type=F32)
    qkv_ref[:, :q_cols] = (z[:, :q_cols] * Q_SCALE).astype(BF16)
    qkv_ref[:, q_cols:] = z[:, q_cols:].astype(BF16)
    rg_ref[...] = jnp.dot(h, wrg_ref[...], preferred_element_type=F32)
    ga_ref[...] = jnp.dot(h, wga_ref[...], preferred_element_type=F32)
    gr_ref[...] = jnp.dot(h, wgr_ref[...], preferred_element_type=F32)


def _in_proj(x2d, g, wqkv, wrg, wga, wgr, q_cols):
    n, d = x2d.shape
    tm = min(FFN_TOKENS, n)
    ws = (wqkv, wrg, wga, wgr)

    def row(w):
        return pl.BlockSpec((tm, w), lambda i: (i, 0))

    return pl.pallas_call(
        functools.partial(_in_proj_kernel, q_cols=q_cols),
        grid=(n // tm,),
        in_specs=[row(d), _resident((1, d))] + [_resident(w.shape) for w in ws],
        out_specs=[row(w.shape[1]) for w in ws],
        out_shape=[jax.ShapeDtypeStruct((n, wqkv.shape[1]), BF16)]
        + [jax.ShapeDtypeStruct((n, w.shape[1]), F32) for w in ws[1:]],
        compiler_params=_params("parallel"),
        name="in_proj",
    )(x2d, g, *ws)


def _diff_attn_kernel(q_ref, k_ref, v_ref, lq1_ref, lk1_ref, lq2_ref, lk2_ref, g_ref,
                      o_ref, qt_ref, vt_ref, m_ref, acc_ref, s_ref, *, blk, lambda_init):
    i = pl.program_id(1)
    n_blocks = v_ref.shape[0] // blk
    heads = range(DA_HEADS)

    @pl.when(i == 0)
    def _():
        for h in heads:
            for c in range(n_blocks):
                vt = v_ref[c * blk:(c + 1) * blk, h * HEAD_W:(h + 1) * HEAD_W].astype(F32).T
                vt_ref[h, c, :HEAD_W, :] = vt.astype(BF16)
                vt_ref[h, c, HEAD_W:, :] = jnp.ones((ONES_ROWS, blk), BF16)

    for h in heads:
        qt = q_ref[:, h * HEAD_W:(h + 1) * HEAD_W].astype(F32).T
        feat = lax.broadcasted_iota(jnp.int32, qt.shape, 0)
        qt_ref[h] = jnp.concatenate([jnp.where(feat < DA_QK, qt, 0.0),
                                     jnp.where(feat >= DA_QK, qt, 0.0)], axis=1).astype(BF16)
    m_ref[...] = jnp.full(m_ref.shape, NEG_INF, F32)
    acc_ref[...] = jnp.zeros(acc_ref.shape, F32)

    def scores(j, h):
        start = pl.multiple_of(j * blk, blk)
        k = k_ref[pl.ds(start, blk), h * HEAD_W:(h + 1) * HEAD_W]
        return jnp.dot(k, qt_ref[h], preferred_element_type=F32)

    def step(j, masked):
        s = s_ref[...]
        for h in heads:
            s_next = None
            if h + 1 < DA_HEADS:
                s_next = scores(j, h + 1)
            elif not masked:
                s_ref[...] = scores(j + 1, 0)
            if masked:
                key = lax.broadcasted_iota(jnp.int32, s.shape, 0)
                qry = lax.broadcasted_iota(jnp.int32, s.shape, 1)
                qry = jnp.where(qry >= blk, qry - blk, qry)
                s = jnp.where(key <= qry, s, NEG_INF)
            m_old = m_ref[h]
            m_new = jnp.maximum(m_old, jnp.max(s, axis=0, keepdims=True))
            alpha = jnp.exp2(m_old - m_new)
            p = jnp.exp2(s - m_new).astype(BF16)
            pv = jnp.dot(vt_ref[h, j], p, preferred_element_type=F32)
            acc_ref[h] = alpha * acc_ref[h] + pv
            m_ref[h] = m_new
            s = s_next

    def body(j, carry):
        step(j, False)
        return carry

    s_ref[...] = scores(0, 0)
    lax.fori_loop(0, i, body, 0)
    step(i, True)

    lam = (jnp.exp(jnp.sum(lq1_ref[...] * lk1_ref[...], axis=-1, keepdims=True))
           - jnp.exp(jnp.sum(lq2_ref[...] * lk2_ref[...], axis=-1, keepdims=True))
           + lambda_init)
    for h in heads:
        acc = acc_ref[h]
        ot = acc[:HEAD_W] / acc[HEAD_W:HEAD_W + 1]
        ot = ot[:, :blk] - lam * ot[:, blk:]
        ot = ot * lax.rsqrt(jnp.mean(ot * ot, axis=0, keepdims=True) + SUBLN_EPS)
        o_ref[:, h * HEAD_W:(h + 1) * HEAD_W] = (
            ot.T * g_ref[...] * (1.0 - lambda_init)).astype(o_ref.dtype)


def _diff_attn(qkv, lq1, lk1, lq2, lk2, subln_g, lambda_init):
    b, t, _ = qkv.shape
    blk = min(ATTN_BLOCK, t)
    width = DA_HEADS * HEAD_W
    small = _resident((1, DA_QK))
    return pl.pallas_call(
        functools.partial(_diff_attn_kernel, blk=blk, lambda_init=lambda_init),
        grid=(b, t // blk),
        in_specs=[
            pl.BlockSpec((None, blk, width), lambda bi, i: (bi, i, 0)),
            pl.BlockSpec((None, t, width), lambda bi, i: (bi, 0, 1)),
            pl.BlockSpec((None, t, width), lambda bi, i: (bi, 0, 2)),
            small, small, small, small, _resident((1, HEAD_W)),
        ],
        out_specs=pl.BlockSpec((None, blk, width), lambda bi, i: (bi, i, 0)),
        out_shape=jax.ShapeDtypeStruct((b, t, width), BF16),
        scratch_shapes=[
            pltpu.VMEM((DA_HEADS, HEAD_W, 2 * blk), BF16),
            pltpu.VMEM((DA_HEADS, t // blk, HEAD_W + ONES_ROWS, blk), BF16),
            pltpu.VMEM((DA_HEADS, 1, 2 * blk), F32),
            pltpu.VMEM((DA_HEADS, HEAD_W + ONES_ROWS, 2 * blk), F32),
            pltpu.VMEM((blk, 2 * blk), F32),
        ],
        compiler_params=_params("parallel", "arbitrary"),
        name="diff_attn",
    )(qkv, qkv, qkv, lq1, lk1, lq2, lk2, subln_g)


def _rotate_every_two(x):
    lane = lax.broadcasted_iota(jnp.int32, x.shape, 1)
    nxt = pltpu.roll(x, HEAD_W - 1, 1)
    prv = pltpu.roll(x, 1, 1)
    return jnp.where(lane % 2 == 0, -nxt, prv)


def _retention_kernel(q_ref, k_ref, v_ref, gate_ref, sin_ref, cos_ref, logg_ref,
                      o_ref, state_ref, *, chunk):
    c = pl.program_id(2)

    @pl.when(c == 0)
    def _():
        state_ref[...] = jnp.zeros(state_ref.shape, F32)

    sin = sin_ref[...]
    cos = cos_ref[...]
    q = q_ref[...].astype(F32)
    k = k_ref[...].astype(F32) * (HEAD_W ** -0.5)
    q = q * cos + _rotate_every_two(q) * sin
    k = k * cos + _rotate_every_two(k) * sin
    v = v_ref[...]

    log_g = logg_ref[...]
    idx = lax.broadcasted_iota(jnp.int32, (chunk, HEAD_W), 0).astype(F32)
    xi = jnp.exp(log_g * (idx + 1.0))
    zeta = jnp.exp(log_g * (chunk - 1.0 - idx))
    qi = lax.broadcasted_iota(jnp.int32, (chunk, chunk), 0)
    ki = lax.broadcasted_iota(jnp.int32, (chunk, chunk), 1)
    diff = (qi - ki).astype(F32)
    dmask = jnp.where(diff >= 0, jnp.exp(log_g[:, :1] * jnp.maximum(diff, 0.0)), 0.0)

    qb = q.astype(BF16)
    s = lax.dot_general(qb, k.astype(BF16), (((1,), (1,)), ((), ())),
                        preferred_element_type=F32) * dmask
    intra = jnp.dot(s.astype(BF16), v, preferred_element_type=F32)
    state = state_ref[...]
    inter = jnp.dot((q * xi).astype(BF16), state.astype(BF16), preferred_element_type=F32)
    kv = lax.dot_general((k * zeta).astype(BF16), v, (((0,), (0,)), ((), ())),
                         preferred_element_type=F32)
    state_ref[...] = jnp.exp(log_g * float(chunk)) * state + kv

    y = intra + inter
    y = y * lax.rsqrt(jnp.mean(y * y, axis=-1, keepdims=True) + NORM_EPS)
    gate = gate_ref[...]
    o_ref[...] = (y * (gate * jax.nn.sigmoid(gate))).astype(o_ref.dtype)


def _retention(qkv, gate, sin, cos, log_g, q_col, k_col, v_col):
    b, t, _ = qkv.shape
    chunk = min(RET_CHUNK, t)

    def slab(col):
        return pl.BlockSpec((None, chunk, HEAD_W), lambda bi, h, c: (bi, c, col + h))

    table = pl.BlockSpec((chunk, HEAD_W), lambda bi, h, c: (c, 0))
    return pl.pallas_call(
        functools.partial(_retention_kernel, chunk=chunk),
        grid=(b, RET_HEADS, t // chunk),
        in_specs=[slab(q_col), slab(k_col), slab(v_col), slab(0), table, table,
                  pl.BlockSpec((None, 1, HEAD_W), lambda bi, h, c: (h, 0, 0))],
        out_specs=slab(0),
        out_shape=jax.ShapeDtypeStruct((b, t, RET_HEADS * HEAD_W), BF16),
        scratch_shapes=[pltpu.VMEM((HEAD_W, HEAD_W), F32)],
        compiler_params=_params("parallel", "parallel", "arbitrary"),
        name="retention",
    )(qkv, qkv, qkv, gate, sin, cos, log_g)


def _merge_kernel(x_ref, ya_ref, yr_ref, ga_ref, gr_ref, wa_ref, wr_ref, wo_ref, g_ref, o_ref):
    ya = jnp.dot(ya_ref[...], wa_ref[...], preferred_element_type=F32)
    yr = jnp.dot(yr_ref[...], wr_ref[...], preferred_element_type=F32)
    merged = jax.nn.sigmoid(ga_ref[...]) * ya + jax.nn.sigmoid(gr_ref[...]) * yr
    y = jnp.dot(merged.astype(BF16), wo_ref[...], preferred_element_type=F32)
    o_ref[...] = x_ref[...] + _rmsnorm(y, g_ref[...], NORM_EPS)


def _merge(x2d, ya, yr, ga, gr, wa, wr, wo, post_g):
    n, d = x2d.shape
    tm = min(FFN_TOKENS, n)

    def row(w):
        return pl.BlockSpec((tm, w), lambda i: (i, 0))

    return pl.pallas_call(
        _merge_kernel,
        grid=(n // tm,),
        in_specs=[row(d), row(ya.shape[1]), row(yr.shape[1]), row(d), row(d),
                  _resident(wa.shape), _resident(wr.shape), _resident(wo.shape), _resident((1, d))],
        out_specs=row(d),
        out_shape=jax.ShapeDtypeStruct((n, d), F32),
        compiler_params=_params("parallel"),
        name="merge",
    )(x2d, ya, yr, ga, gr, wa, wr, wo, post_g)


def _retention_tables(t):
    angle = 1.0 / (10000.0 ** jnp.linspace(0.0, 1.0, HEAD_W // 2, dtype=F32))
    angle = jnp.repeat(angle, 2)
    ang = jnp.arange(t, dtype=F32)[:, None] * angle[None, :]
    log_g = jnp.log(1.0 - 2.0 ** (-5.0 - jnp.arange(RET_HEADS, dtype=F32)))
    log_g = jnp.broadcast_to(log_g[:, None, None], (RET_HEADS, 1, HEAD_W))
    return jnp.sin(ang), jnp.cos(ang), log_g


def kernel(x, ffn1_pre_g, ffn1_w_gate, ffn1_w_up, ffn1_w_down, ffn1_post_g, mix_pre_g, w_in, lambda_q1, lambda_k1, lambda_q2, lambda_k2, diff_subln_g, w_attn_proj, w_ret_proj, w_out, mix_post_g, ffn2_pre_g, ffn2_w_gate, ffn2_w_up, ffn2_w_down, ffn2_post_g):
    b, t, d = x.shape
    depth = w_in.shape[0]
    n = b * t
    da_w = DA_HEADS * HEAD_W
    ret_w = RET_HEADS * HEAD_W
    qkv_w = 3 * da_w + 3 * ret_w
    sin, cos, log_g = _retention_tables(t)
    bf = lambda w: w.astype(BF16)

    x2d = x.reshape(n, d)
    for l in range(depth):
        lambda_init = 0.8 - 0.6 * math.exp(-0.3 * l)
        x2d = _ffn(x2d, ffn1_pre_g[l][None], bf(ffn1_w_gate[l]), bf(ffn1_w_up[l]),
                   bf(ffn1_w_down[l]), ffn1_post_g[l][None])

        w = bf(w_in[l])
        qkv, rg, ga, gr = _in_proj(
            x2d, mix_pre_g[l][None], w[:, :qkv_w], w[:, qkv_w:qkv_w + ret_w],
            w[:, qkv_w + ret_w:qkv_w + ret_w + d], w[:, qkv_w + ret_w + d:], q_cols=da_w)
        qkv = qkv.reshape(b, t, qkv_w)
        ya = _diff_attn(qkv, lambda_q1[l][None], lambda_k1[l][None], lambda_q2[l][None],
                        lambda_k2[l][None], diff_subln_g[l][None], lambda_init)
        yr = _retention(qkv, rg.reshape(b, t, ret_w), sin, cos, log_g,
                        q_col=3 * DA_HEADS, k_col=3 * DA_HEADS + RET_HEADS,
                        v_col=3 * DA_HEADS + 2 * RET_HEADS)
        x2d = _merge(x2d, ya.reshape(n, da_w), yr.reshape(n, ret_w), ga, gr,
                     bf(w_attn_proj[l]), bf(w_ret_proj[l]), bf(w_out[l]), mix_post_g[l][None])

        x2d = _ffn(x2d, ffn2_pre_g[l][None], bf(ffn2_w_gate[l]), bf(ffn2_w_up[l]),
                   bf(ffn2_w_down[l]), ffn2_post_g[l][None])
    return x2d.reshape(b, t, d)
```

```python
import functools
import math

import jax
import jax.numpy as jnp
from jax import lax
from jax.experimental import pallas as pl
from jax.experimental.pallas import tpu as pltpu

F32 = jnp.float32
BF16 = jnp.bfloat16

DA_HEADS = 4
DA_QK = 64
HEAD_W = 128
RET_HEADS = 4
NORM_EPS = 1e-6
SUBLN_EPS = 1e-5
NEG_INF = -1e30
Q_SCALE = DA_QK ** -0.5 * math.log2(math.e)
ONES_ROWS = 16

VMEM_LIMIT_BYTES = 56 * 1024 * 1024

FFN_TOKENS = 512
ATTN_BLOCK = 256
SCORE_LOOKAHEAD = 2
RET_CHUNK = 256


def _params(*sem):
    return pltpu.CompilerParams(dimension_semantics=sem, vmem_limit_bytes=VMEM_LIMIT_BYTES)


def _resident(shape):
    nd = len(shape)
    return pl.BlockSpec(shape, lambda *_: (0,) * nd, pipeline_mode=pl.Buffered(1))


def _rmsnorm(x, g, eps):
    return x * lax.rsqrt(jnp.mean(x * x, axis=-1, keepdims=True) + eps) * g


def _ffn_kernel(x_ref, pre_g_ref, wg_ref, wu_ref, wd_ref, post_g_ref, o_ref):
    x = x_ref[...]
    h = _rmsnorm(x, pre_g_ref[...], NORM_EPS).astype(BF16)
    g = jnp.dot(h, wg_ref[...], preferred_element_type=F32)
    u = jnp.dot(h, wu_ref[...], preferred_element_type=F32)
    a = (g * jax.nn.sigmoid(g) * u).astype(BF16)
    y = jnp.dot(a, wd_ref[...], preferred_element_type=F32)
    o_ref[...] = x + 0.5 * _rmsnorm(y, post_g_ref[...], NORM_EPS)


def _ffn(x2d, pre_g, wg, wu, wd, post_g):
    n, d = x2d.shape
    f = wg.shape[1]
    tm = min(FFN_TOKENS, n)
    row = pl.BlockSpec((tm, d), lambda i: (i, 0))
    return pl.pallas_call(
        _ffn_kernel,
        grid=(n // tm,),
        in_specs=[row, _resident((1, d)), _resident((d, f)), _resident((d, f)),
                  _resident((f, d)), _resident((1, d))],
        out_specs=row,
        out_shape=jax.ShapeDtypeStruct((n, d), F32),
        compiler_params=_params("parallel"),
        name="ffn",
    )(x2d, pre_g, wg, wu, wd, post_g)


def _in_proj_kernel(x_ref, g_ref, wqkv_ref, wrg_ref, wga_ref, wgr_ref,
                    qkv_ref, rg_ref, ga_ref, gr_ref, *, q_cols):
    h = _rmsnorm(x_ref[...], g_ref[...], NORM_EPS).astype(BF16)
    z = jnp.dot(h, wqkv_ref[...], preferred_element_type=F32)
    qkv_ref[:, :q_cols] = (z[:, :q_cols] * Q_SCALE).astype(BF16)
    qkv_ref[:, q_cols:] = z[:, q_cols:].astype(BF16)
    rg_ref[...] = jnp.dot(h, wrg_ref[...], preferred_element_type=F32)
    ga_ref[...] = jnp.dot(h, wga_ref[...], preferred_element_type=F32)
    gr_ref[...] = jnp.dot(h, wgr_ref[...], preferred_element_type=F32)


def _in_proj(x2d, g, wqkv, wrg, wga, wgr, q_cols):
    n, d = x2d.shape
    tm = min(FFN_TOKENS, n)
    ws = (wqkv, wrg, wga, wgr)

    def row(w):
        return pl.BlockSpec((tm, w), lambda i: (i, 0))

    return pl.pallas_call(
        functools.partial(_in_proj_kernel, q_cols=q_cols),
        grid=(n // tm,),
        in_specs=[row(d), _resident((1, d))] + [_resident(w.shape) for w in ws],
        out_specs=[row(w.shape[1]) for w in ws],
        out_shape=[jax.ShapeDtypeStruct((n, wqkv.shape[1]), BF16)]
        + [jax.ShapeDtypeStruct((n, w.shape[1]), F32) for w in ws[1:]],
        compiler_params=_params("parallel"),
        name="in_proj",
    )(x2d, g, *ws)


def _diff_attn_kernel(q_ref, k_ref, v_ref, lq1_ref, lk1_ref, lq2_ref, lk2_ref, g_ref,
                      o_ref, qt_ref, vt_ref, m_ref, acc_ref, s_ref, *, blk, lambda_init):
    i = pl.program_id(1)
    n_blocks = v_ref.shape[0] // blk
    heads = range(DA_HEADS)

    @pl.when(i == 0)
    def _():
        for h in heads:
            for c in range(n_blocks):
                vt = v_ref[c * blk:(c + 1) * blk, h * HEAD_W:(h + 1) * HEAD_W].astype(F32).T
                vt_ref[h, c, :HEAD_W, :] = vt.astype(BF16)
                vt_ref[h, c, HEAD_W:, :] = jnp.ones((ONES_ROWS, blk), BF16)

    for h in heads:
        qt = q_ref[:, h * HEAD_W:(h + 1) * HEAD_W].astype(F32).T
        feat = lax.broadcasted_iota(jnp.int32, qt.shape, 0)
        qt_ref[h] = jnp.concatenate([jnp.where(feat < DA_QK, qt, 0.0),
                                     jnp.where(feat >= DA_QK, qt, 0.0)], axis=1).astype(BF16)
    m_ref[...] = jnp.full(m_ref.shape, NEG_INF, F32)
    acc_ref[...] = jnp.zeros(acc_ref.shape, F32)

    def scores(j, h):
        start = pl.multiple_of(j * blk, blk)
        k = k_ref[pl.ds(start, blk), h * HEAD_W:(h + 1) * HEAD_W]
        return jnp.dot(k, qt_ref[h], preferred_element_type=F32)

    def step(j, masked):
        ahead = {}
        for h in heads:
            t = h + SCORE_LOOKAHEAD
            if t < DA_HEADS:
                ahead[t] = scores(j, t)
            elif not masked:
                s_ref[t - DA_HEADS] = scores(j + 1, t - DA_HEADS)
            s = s_ref[h] if h < SCORE_LOOKAHEAD else ahead.pop(h)
            if masked:
                key = lax.broadcasted_iota(jnp.int32, s.shape, 0)
                qry = lax.broadcasted_iota(jnp.int32, s.shape, 1)
                qry = jnp.where(qry >= blk, qry - blk, qry)
                s = jnp.where(key <= qry, s, NEG_INF)
            m_old = m_ref[h]
            m_new = jnp.maximum(m_old, jnp.max(s, axis=0, keepdims=True))
            alpha = jnp.exp2(m_old - m_new)
            p = jnp.exp2(s - m_new).astype(BF16)
            pv = jnp.dot(vt_ref[h, j], p, preferred_element_type=F32)
            acc_ref[h] = alpha * acc_ref[h] + pv
            m_ref[h] = m_new

    def body(j, carry):
        step(j, False)
        return carry

    for h in range(SCORE_LOOKAHEAD):
        s_ref[h] = scores(0, h)
    lax.fori_loop(0, i, body, 0)
    step(i, True)

    lam = (jnp.exp(jnp.sum(lq1_ref[...] * lk1_ref[...], axis=-1, keepdims=True))
           - jnp.exp(jnp.sum(lq2_ref[...] * lk2_ref[...], axis=-1, keepdims=True))
           + lambda_init)
    for h in heads:
        acc = acc_ref[h]
        ot = acc[:HEAD_W] / acc[HEAD_W:HEAD_W + 1]
        ot = ot[:, :blk] - lam * ot[:, blk:]
        ot = ot * lax.rsqrt(jnp.mean(ot * ot, axis=0, keepdims=True) + SUBLN_EPS)
        o_ref[:, h * HEAD_W:(h + 1) * HEAD_W] = (
            ot.T * g_ref[...] * (1.0 - lambda_init)).astype(o_ref.dtype)


def _diff_attn(qkv, lq1, lk1, lq2, lk2, subln_g, lambda_init):
    b, t, _ = qkv.shape
    blk = min(ATTN_BLOCK, t)
    width = DA_HEADS * HEAD_W
    small = _resident((1, DA_QK))
    return pl.pallas_call(
        functools.partial(_diff_attn_kernel, blk=blk, lambda_init=lambda_init),
        grid=(b, t // blk),
        in_specs=[
            pl.BlockSpec((None, blk, width), lambda bi, i: (bi, i, 0)),
            pl.BlockSpec((None, t, width), lambda bi, i: (bi, 0, 1)),
            pl.BlockSpec((None, t, width), lambda bi, i: (bi, 0, 2)),
            small, small, small, small, _resident((1, HEAD_W)),
        ],
        out_specs=pl.BlockSpec((None, blk, width), lambda bi, i: (bi, i, 0)),
        out_shape=jax.ShapeDtypeStruct((b, t, width), BF16),
        scratch_shapes=[
            pltpu.VMEM((DA_HEADS, HEAD_W, 2 * blk), BF16),
            pltpu.VMEM((DA_HEADS, t // blk, HEAD_W + ONES_ROWS, blk), BF16),
            pltpu.VMEM((DA_HEADS, 1, 2 * blk), F32),
            pltpu.VMEM((DA_HEADS, HEAD_W + ONES_ROWS, 2 * blk), F32),
            pltpu.VMEM((SCORE_LOOKAHEAD, blk, 2 * blk), F32),
        ],
        compiler_params=_params("parallel", "arbitrary"),
        name="diff_attn",
    )(qkv, qkv, qkv, lq1, lk1, lq2, lk2, subln_g)


def _rotate_every_two(x):
    lane = lax.broadcasted_iota(jnp.int32, x.shape, 1)
    nxt = pltpu.roll(x, HEAD_W - 1, 1)
    prv = pltpu.roll(x, 1, 1)
    return jnp.where(lane % 2 == 0, -nxt, prv)


def _retention_kernel(q_ref, k_ref, v_ref, gate_ref, sin_ref, cos_ref, logg_ref,
                      o_ref, state_ref, dmask_ref, xi_ref, zeta_ref, *, chunk):
    c = pl.program_id(1)
    heads = range(RET_HEADS)
    k_scale = HEAD_W ** -0.5

    @pl.when(c == 0)
    def _():
        state_ref[...] = jnp.zeros(state_ref.shape, F32)
        idx = lax.broadcasted_iota(jnp.int32, (chunk, HEAD_W), 0).astype(F32)
        qi = lax.broadcasted_iota(jnp.int32, (chunk, chunk), 0)
        ki = lax.broadcasted_iota(jnp.int32, (chunk, chunk), 1)
        diff = (qi - ki).astype(F32)
        for h in heads:
            log_g = logg_ref[h]
            xi_ref[h] = jnp.exp(log_g * (idx + 1.0))
            zeta_ref[h] = jnp.exp(log_g * (chunk - 1.0 - idx)) * k_scale
            dmask_ref[h] = jnp.where(
                diff >= 0, jnp.exp(log_g[:, :1] * jnp.maximum(diff, 0.0)), 0.0) * k_scale

    sin = sin_ref[...]
    cos = cos_ref[...]
    for h in heads:
        cols = slice(h * HEAD_W, (h + 1) * HEAD_W)
        q = q_ref[:, cols].astype(F32)
        k = k_ref[:, cols].astype(F32)
        q = q * cos + _rotate_every_two(q) * sin
        k = k * cos + _rotate_every_two(k) * sin
        v = v_ref[:, cols]

        s = lax.dot_general(q.astype(BF16), k.astype(BF16), (((1,), (1,)), ((), ())),
                            preferred_element_type=F32) * dmask_ref[h]
        intra = jnp.dot(s.astype(BF16), v, preferred_element_type=F32)
        state = state_ref[h]
        inter = jnp.dot((q * xi_ref[h]).astype(BF16), state.astype(BF16),
                        preferred_element_type=F32)
        kv = lax.dot_general((k * zeta_ref[h]).astype(BF16), v, (((0,), (0,)), ((), ())),
                             preferred_element_type=F32)
        state_ref[h] = jnp.exp(logg_ref[h] * float(chunk)) * state + kv

        y = intra + inter
        y = y * lax.rsqrt(jnp.mean(y * y, axis=-1, keepdims=True) + NORM_EPS)
        gate = gate_ref[:, cols]
        o_ref[:, cols] = (y * (gate * jax.nn.sigmoid(gate))).astype(o_ref.dtype)


def _retention(qkv, gate, sin, cos, log_g, q_col, k_col, v_col):
    b, t, _ = qkv.shape
    chunk = min(RET_CHUNK, t)
    width = RET_HEADS * HEAD_W

    def slab(col):
        return pl.BlockSpec((None, chunk, width), lambda bi, c: (bi, c, col))

    table = pl.BlockSpec((chunk, HEAD_W), lambda bi, c: (c, 0))
    return pl.pallas_call(
        functools.partial(_retention_kernel, chunk=chunk),
        grid=(b, t // chunk),
        in_specs=[slab(q_col), slab(k_col), slab(v_col), slab(0), table, table,
                  _resident(log_g.shape)],
        out_specs=slab(0),
        out_shape=jax.ShapeDtypeStruct((b, t, width), BF16),
        scratch_shapes=[pltpu.VMEM((RET_HEADS, HEAD_W, HEAD_W), F32),
                        pltpu.VMEM((RET_HEADS, chunk, chunk), F32),
                        pltpu.VMEM((RET_HEADS, chunk, HEAD_W), F32),
                        pltpu.VMEM((RET_HEADS, chunk, HEAD_W), F32)],
        compiler_params=_params("parallel", "arbitrary"),
        name="retention",
    )(qkv, qkv, qkv, gate, sin, cos, log_g)


def _merge_kernel(x_ref, ya_ref, yr_ref, ga_ref, gr_ref, wa_ref, wr_ref, wo_ref, g_ref, o_ref):
    ya = jnp.dot(ya_ref[...], wa_ref[...], preferred_element_type=F32)
    yr = jnp.dot(yr_ref[...], wr_ref[...], preferred_element_type=F32)
    merged = jax.nn.sigmoid(ga_ref[...]) * ya + jax.nn.sigmoid(gr_ref[...]) * yr
    y = jnp.dot(merged.astype(BF16), wo_ref[...], preferred_element_type=F32)
    o_ref[...] = x_ref[...] + _rmsnorm(y, g_ref[...], NORM_EPS)


def _merge(x2d, ya, yr, ga, gr, wa, wr, wo, post_g):
    n, d = x2d.shape
    tm = min(FFN_TOKENS, n)

    def row(w):
        return pl.BlockSpec((tm, w), lambda i: (i, 0))

    return pl.pallas_call(
        _merge_kernel,
        grid=(n // tm,),
        in_specs=[row(d), row(ya.shape[1]), row(yr.shape[1]), row(d), row(d),
                  _resident(wa.shape), _resident(wr.shape), _resident(wo.shape), _resident((1, d))],
        out_specs=row(d),
        out_shape=jax.ShapeDtypeStruct((n, d), F32),
        compiler_params=_params("parallel"),
        name="merge",
    )(x2d, ya, yr, ga, gr, wa, wr, wo, post_g)


def _retention_tables(t):
    angle = 1.0 / (10000.0 ** jnp.linspace(0.0, 1.0, HEAD_W // 2, dtype=F32))
    angle = jnp.repeat(angle, 2)
    ang = jnp.arange(t, dtype=F32)[:, None] * angle[None, :]
    log_g = jnp.log(1.0 - 2.0 ** (-5.0 - jnp.arange(RET_HEADS, dtype=F32)))
    log_g = jnp.broadcast_to(log_g[:, None, None], (RET_HEADS, 1, HEAD_W))
    return jnp.sin(ang), jnp.cos(ang), log_g


def kernel(x, ffn1_pre_g, ffn1_w_gate, ffn1_w_up, ffn1_w_down, ffn1_post_g, mix_pre_g, w_in, lambda_q1, lambda_k1, lambda_q2, lambda_k2, diff_subln_g, w_attn_proj, w_ret_proj, w_out, mix_post_g, ffn2_pre_g, ffn2_w_gate, ffn2_w_up, ffn2_w_down, ffn2_post_g):
    b, t, d = x.shape
    depth = w_in.shape[0]
    n = b * t
    da_w = DA_HEADS * HEAD_W
    ret_w = RET_HEADS * HEAD_W
    qkv_w = 3 * da_w + 3 * ret_w
    sin, cos, log_g = _retention_tables(t)
    bf = lambda w: w.astype(BF16)

    x2d = x.reshape(n, d)
    for l in range(depth):
        lambda_init = 0.8 - 0.6 * math.exp(-0.3 * l)
        x2d = _ffn(x2d, ffn1_pre_g[l][None], bf(ffn1_w_gate[l]), bf(ffn1_w_up[l]),
                   bf(ffn1_w_down[l]), ffn1_post_g[l][None])

        w = bf(w_in[l])
        qkv, rg, ga, gr = _in_proj(
            x2d, mix_pre_g[l][None], w[:, :qkv_w], w[:, qkv_w:qkv_w + ret_w],
            w[:, qkv_w + ret_w:qkv_w + ret_w + d], w[:, qkv_w + ret_w + d:], q_cols=da_w)
        qkv = qkv.reshape(b, t, qkv_w)
        ya = _diff_attn(qkv, lambda_q1[l][None], lambda_k1[l][None], lambda_q2[l][None],
                        lambda_k2[l][None], diff_subln_g[l][None], lambda_init)
        yr = _retention(qkv, rg.reshape(b, t, ret_w), sin, cos, log_g,
                        q_col=3, k_col=4, v_col=5)
        x2d = _merge(x2d, ya.reshape(n, da_w), yr.reshape(n, ret_w), ga, gr,
                     bf(w_attn_proj[l]), bf(w_ret_proj[l]), bf(w_out[l]), mix_post_g[l][None])

        x2d = _ffn(x2d, ffn2_pre_g[l][None], bf(ffn2_w_gate[l]), bf(ffn2_w_up[l]),
                   bf(ffn2_w_down[l]), ffn2_post_g[l][None])
    return x2d.reshape(b, t, d)
```

```python
import functools
import math

import jax
import jax.numpy as jnp
from jax import lax
from jax.experimental import pallas as pl
from jax.experimental.pallas import tpu as pltpu

F32 = jnp.float32
BF16 = jnp.bfloat16
GATE_DTYPE = jnp.bfloat16

DA_HEADS = 4
DA_QK = 64
HEAD_W = 128
RET_HEADS = 4
NORM_EPS = 1e-6
SUBLN_EPS = 1e-5
NEG_INF = -1e30
Q_SCALE = DA_QK ** -0.5 * math.log2(math.e)
ONES_ROWS = 16

VMEM_LIMIT_BYTES = 56 * 1024 * 1024

FFN_TOKENS = 512
ROW_GROUPS = 2
ATTN_BLOCK = 256
SCORE_LOOKAHEAD = 2
RET_CHUNK = 256


def _params(*sem):
    return pltpu.CompilerParams(dimension_semantics=sem, vmem_limit_bytes=VMEM_LIMIT_BYTES)


def _resident(shape):
    nd = len(shape)
    return pl.BlockSpec(shape, lambda *_: (0,) * nd, pipeline_mode=pl.Buffered(1))


def _sigmoid(x):
    return 0.5 * jnp.tanh(0.5 * x) + 0.5


def _rmsnorm(x, g, eps):
    return x * lax.rsqrt(jnp.mean(x * x, axis=-1, keepdims=True) + eps) * g


def _ffn_kernel(x_ref, pre_g_ref, wg_ref, wu_ref, wd_ref, post_g_ref, o_ref):
    rows = x_ref.shape[0] // ROW_GROUPS
    for r in range(ROW_GROUPS):
        sl = slice(r * rows, (r + 1) * rows)
        x = x_ref[sl, :]
        h = _rmsnorm(x, pre_g_ref[...], NORM_EPS).astype(BF16)
        g = jnp.dot(h, wg_ref[...], preferred_element_type=F32)
        u = jnp.dot(h, wu_ref[...], preferred_element_type=F32)
        a = (g * _sigmoid(g) * u).astype(BF16)
        y = jnp.dot(a, wd_ref[...], preferred_element_type=F32)
        o_ref[sl, :] = x + 0.5 * _rmsnorm(y, post_g_ref[...], NORM_EPS)


def _ffn(x2d, pre_g, wg, wu, wd, post_g):
    n, d = x2d.shape
    f = wg.shape[1]
    tm = min(FFN_TOKENS, n)
    row = pl.BlockSpec((tm, d), lambda i: (i, 0))
    return pl.pallas_call(
        _ffn_kernel,
        grid=(n // tm,),
        in_specs=[row, _resident((1, d)), _resident((d, f)), _resident((d, f)),
                  _resident((f, d)), _resident((1, d))],
        out_specs=row,
        out_shape=jax.ShapeDtypeStruct((n, d), F32),
        compiler_params=_params("parallel"),
        name="ffn",
    )(x2d, pre_g, wg, wu, wd, post_g)


def _in_proj_kernel(x_ref, g_ref, wqkv_ref, wrg_ref, wga_ref, wgr_ref,
                    qkv_ref, rg_ref, ga_ref, gr_ref, *, q_cols):
    rows = x_ref.shape[0] // ROW_GROUPS
    for r in range(ROW_GROUPS):
        sl = slice(r * rows, (r + 1) * rows)
        h = _rmsnorm(x_ref[sl, :], g_ref[...], NORM_EPS).astype(BF16)
        rg = jnp.dot(h, wrg_ref[...], preferred_element_type=F32)
        rg_ref[sl, :] = (rg * _sigmoid(rg)).astype(rg_ref.dtype)
        ga = jnp.dot(h, wga_ref[...], preferred_element_type=F32)
        ga_ref[sl, :] = _sigmoid(ga).astype(ga_ref.dtype)
        gr = jnp.dot(h, wgr_ref[...], preferred_element_type=F32)
        gr_ref[sl, :] = _sigmoid(gr).astype(gr_ref.dtype)
        z = jnp.dot(h, wqkv_ref[...], preferred_element_type=F32)
        qkv_ref[sl, :q_cols] = (z[:, :q_cols] * Q_SCALE).astype(BF16)
        qkv_ref[sl, q_cols:] = z[:, q_cols:].astype(BF16)


def _in_proj(x2d, g, wqkv, wrg, wga, wgr, q_cols):
    n, d = x2d.shape
    tm = min(FFN_TOKENS, n)
    ws = (wqkv, wrg, wga, wgr)

    def row(w):
        return pl.BlockSpec((tm, w), lambda i: (i, 0))

    return pl.pallas_call(
        functools.partial(_in_proj_kernel, q_cols=q_cols),
        grid=(n // tm,),
        in_specs=[row(d), _resident((1, d))] + [_resident(w.shape) for w in ws],
        out_specs=[row(w.shape[1]) for w in ws],
        out_shape=[jax.ShapeDtypeStruct((n, wqkv.shape[1]), BF16)]
        + [jax.ShapeDtypeStruct((n, w.shape[1]), GATE_DTYPE) for w in ws[1:]],
        compiler_params=_params("parallel"),
        name="in_proj",
    )(x2d, g, *ws)


def _diff_attn_kernel(q_ref, k_ref, v_ref, lq1_ref, lk1_ref, lq2_ref, lk2_ref, g_ref,
                      o_ref, qt_ref, vt_ref, m_ref, acc_ref, s_ref, *, blk, lambda_init):
    i = pl.program_id(1)
    n_blocks = v_ref.shape[0] // blk
    heads = range(DA_HEADS)

    @pl.when(i == 0)
    def _():
        for h in heads:
            for c in range(n_blocks):
                vt = v_ref[c * blk:(c + 1) * blk, h * HEAD_W:(h + 1) * HEAD_W].astype(F32).T
                vt_ref[h, c, :HEAD_W, :] = vt.astype(BF16)
                vt_ref[h, c, HEAD_W:, :] = jnp.ones((ONES_ROWS, blk), BF16)

    for h in heads:
        qt = q_ref[:, h * HEAD_W:(h + 1) * HEAD_W].astype(F32).T
        feat = lax.broadcasted_iota(jnp.int32, qt.shape, 0)
        qt_ref[h] = jnp.concatenate([jnp.where(feat < DA_QK, qt, 0.0),
                                     jnp.where(feat >= DA_QK, qt, 0.0)], axis=1).astype(BF16)
    m_ref[...] = jnp.full(m_ref.shape, NEG_INF, F32)
    acc_ref[...] = jnp.zeros(acc_ref.shape, F32)

    def scores(j, h):
        start = pl.multiple_of(j * blk, blk)
        k = k_ref[pl.ds(start, blk), h * HEAD_W:(h + 1) * HEAD_W]
        return jnp.dot(k, qt_ref[h], preferred_element_type=F32)

    def step(j, masked):
        ahead = {}
        for h in heads:
            t = h + SCORE_LOOKAHEAD
            if t < DA_HEADS:
                ahead[t] = scores(j, t)
            elif not masked:
                s_ref[t - DA_HEADS] = scores(j + 1, t - DA_HEADS)
            s = s_ref[h] if h < SCORE_LOOKAHEAD else ahead.pop(h)
            if masked:
                key = lax.broadcasted_iota(jnp.int32, s.shape, 0)
                qry = lax.broadcasted_iota(jnp.int32, s.shape, 1)
                qry = jnp.where(qry >= blk, qry - blk, qry)
                s = jnp.where(key <= qry, s, NEG_INF)
            m_old = m_ref[h]
            m_new = jnp.maximum(m_old, jnp.max(s, axis=0, keepdims=True))
            alpha = jnp.exp2(m_old - m_new)
            p = jnp.exp2(s - m_new).astype(BF16)
            pv = jnp.dot(vt_ref[h, j], p, preferred_element_type=F32)
            acc_ref[h] = alpha * acc_ref[h] + pv
            m_ref[h] = m_new

    def body(j, carry):
        step(j, False)
        return carry

    for h in range(SCORE_LOOKAHEAD):
        s_ref[h] = scores(0, h)
    lax.fori_loop(0, i, body, 0)
    step(i, True)

    lam = (jnp.exp(jnp.sum(lq1_ref[...] * lk1_ref[...], axis=-1, keepdims=True))
           - jnp.exp(jnp.sum(lq2_ref[...] * lk2_ref[...], axis=-1, keepdims=True))
           + lambda_init)
    for h in heads:
        acc = acc_ref[h]
        ot = acc[:HEAD_W] / acc[HEAD_W:HEAD_W + 1]
        ot = ot[:, :blk] - lam * ot[:, blk:]
        ot = ot * lax.rsqrt(jnp.mean(ot * ot, axis=0, keepdims=True) + SUBLN_EPS)
        o_ref[:, h * HEAD_W:(h + 1) * HEAD_W] = (
            ot.T * g_ref[...] * (1.0 - lambda_init)).astype(o_ref.dtype)


def _diff_attn(qkv, lq1, lk1, lq2, lk2, subln_g, lambda_init):
    b, t, _ = qkv.shape
    blk = min(ATTN_BLOCK, t)
    width = DA_HEADS * HEAD_W
    small = _resident((1, DA_QK))
    return pl.pallas_call(
        functools.partial(_diff_attn_kernel, blk=blk, lambda_init=lambda_init),
        grid=(b, t // blk),
        in_specs=[
            pl.BlockSpec((None, blk, width), lambda bi, i: (bi, i, 0)),
            pl.BlockSpec((None, t, width), lambda bi, i: (bi, 0, 1)),
            pl.BlockSpec((None, t, width), lambda bi, i: (bi, 0, 2)),
            small, small, small, small, _resident((1, HEAD_W)),
        ],
        out_specs=pl.BlockSpec((None, blk, width), lambda bi, i: (bi, i, 0)),
        out_shape=jax.ShapeDtypeStruct((b, t, width), BF16),
        scratch_shapes=[
            pltpu.VMEM((DA_HEADS, HEAD_W, 2 * blk), BF16),
            pltpu.VMEM((DA_HEADS, t // blk, HEAD_W + ONES_ROWS, blk), BF16),
            pltpu.VMEM((DA_HEADS, 1, 2 * blk), F32),
            pltpu.VMEM((DA_HEADS, HEAD_W + ONES_ROWS, 2 * blk), F32),
            pltpu.VMEM((SCORE_LOOKAHEAD, blk, 2 * blk), F32),
        ],
        compiler_params=_params("parallel", "arbitrary"),
        name="diff_attn",
    )(qkv, qkv, qkv, lq1, lk1, lq2, lk2, subln_g)


def _rotate_every_two(x):
    lane = lax.broadcasted_iota(jnp.int32, x.shape, 1)
    nxt = pltpu.roll(x, HEAD_W - 1, 1)
    prv = pltpu.roll(x, 1, 1)
    return jnp.where(lane % 2 == 0, -nxt, prv)


def _retention_kernel(q_ref, k_ref, v_ref, gate_ref, sin_ref, cos_ref, logg_ref,
                      o_ref, state_ref, dmask_ref, xi_ref, zeta_ref, *, chunk):
    c = pl.program_id(1)
    heads = range(RET_HEADS)
    k_scale = HEAD_W ** -0.5

    @pl.when(c == 0)
    def _():
        state_ref[...] = jnp.zeros(state_ref.shape, F32)
        idx = lax.broadcasted_iota(jnp.int32, (chunk, HEAD_W), 0).astype(F32)
        qi = lax.broadcasted_iota(jnp.int32, (chunk, chunk), 0)
        ki = lax.broadcasted_iota(jnp.int32, (chunk, chunk), 1)
        diff = (qi - ki).astype(F32)
        for h in heads:
            log_g = logg_ref[h]
            xi_ref[h] = jnp.exp(log_g * (idx + 1.0))
            zeta_ref[h] = jnp.exp(log_g * (chunk - 1.0 - idx)) * k_scale
            dmask_ref[h] = jnp.where(
                diff >= 0, jnp.exp(log_g[:, :1] * jnp.maximum(diff, 0.0)), 0.0) * k_scale

    sin = sin_ref[...]
    cos = cos_ref[...]
    for h in heads:
        cols = slice(h * HEAD_W, (h + 1) * HEAD_W)
        q = q_ref[:, cols].astype(F32)
        k = k_ref[:, cols].astype(F32)
        q = q * cos + _rotate_every_two(q) * sin
        k = k * cos + _rotate_every_two(k) * sin
        v = v_ref[:, cols]

        s = lax.dot_general(q.astype(BF16), k.astype(BF16), (((1,), (1,)), ((), ())),
                            preferred_element_type=F32) * dmask_ref[h]
        intra = jnp.dot(s.astype(BF16), v, preferred_element_type=F32)
        state = state_ref[h]
        inter = jnp.dot((q * xi_ref[h]).astype(BF16), state.astype(BF16),
                        preferred_element_type=F32)
        kv = lax.dot_general((k * zeta_ref[h]).astype(BF16), v, (((0,), (0,)), ((), ())),
                             preferred_element_type=F32)
        state_ref[h] = jnp.exp(logg_ref[h] * float(chunk)) * state + kv

        y = intra + inter
        y = y * lax.rsqrt(jnp.mean(y * y, axis=-1, keepdims=True) + NORM_EPS)
        o_ref[:, cols] = (y * gate_ref[:, cols].astype(F32)).astype(o_ref.dtype)


def _retention(qkv, gate, sin, cos, log_g, q_col, k_col, v_col):
    b, t, _ = qkv.shape
    chunk = min(RET_CHUNK, t)
    width = RET_HEADS * HEAD_W

    def slab(col):
        return pl.BlockSpec((None, chunk, width), lambda bi, c: (bi, c, col))

    table = pl.BlockSpec((chunk, HEAD_W), lambda bi, c: (c, 0))
    return pl.pallas_call(
        functools.partial(_retention_kernel, chunk=chunk),
        grid=(b, t // chunk),
        in_specs=[slab(q_col), slab(k_col), slab(v_col), slab(0), table, table,
                  _resident(log_g.shape)],
        out_specs=slab(0),
        out_shape=jax.ShapeDtypeStruct((b, t, width), BF16),
        scratch_shapes=[pltpu.VMEM((RET_HEADS, HEAD_W, HEAD_W), F32),
                        pltpu.VMEM((RET_HEADS, chunk, chunk), F32),
                        pltpu.VMEM((RET_HEADS, chunk, HEAD_W), F32),
                        pltpu.VMEM((RET_HEADS, chunk, HEAD_W), F32)],
        compiler_params=_params("parallel", "arbitrary"),
        name="retention",
    )(qkv, qkv, qkv, gate, sin, cos, log_g)


def _merge_kernel(x_ref, ya_ref, yr_ref, ga_ref, gr_ref, wa_ref, wr_ref, wo_ref, g_ref, o_ref):
    ya = jnp.dot(ya_ref[...], wa_ref[...], preferred_element_type=F32)
    yr = jnp.dot(yr_ref[...], wr_ref[...], preferred_element_type=F32)
    merged = ga_ref[...].astype(F32) * ya + gr_ref[...].astype(F32) * yr
    y = jnp.dot(merged.astype(BF16), wo_ref[...], preferred_element_type=F32)
    o_ref[...] = x_ref[...] + _rmsnorm(y, g_ref[...], NORM_EPS)


def _merge(x2d, ya, yr, ga, gr, wa, wr, wo, post_g):
    n, d = x2d.shape
    tm = min(FFN_TOKENS, n)

    def row(w):
        return pl.BlockSpec((tm, w), lambda i: (i, 0))

    return pl.pallas_call(
        _merge_kernel,
        grid=(n // tm,),
        in_specs=[row(d), row(ya.shape[1]), row(yr.shape[1]), row(d), row(d),
                  _resident(wa.shape), _resident(wr.shape), _resident(wo.shape), _resident((1, d))],
        out_specs=row(d),
        out_shape=jax.ShapeDtypeStruct((n, d), F32),
        compiler_params=_params("parallel"),
        name="merge",
    )(x2d, ya, yr, ga, gr, wa, wr, wo, post_g)


def _retention_tables(t):
    angle = 1.0 / (10000.0 ** jnp.linspace(0.0, 1.0, HEAD_W // 2, dtype=F32))
    angle = jnp.repeat(angle, 2)
    ang = jnp.arange(t, dtype=F32)[:, None] * angle[None, :]
    log_g = jnp.log(1.0 - 2.0 ** (-5.0 - jnp.arange(RET_HEADS, dtype=F32)))
    log_g = jnp.broadcast_to(log_g[:, None, None], (RET_HEADS, 1, HEAD_W))
    return jnp.sin(ang), jnp.cos(ang), log_g


def kernel(x, ffn1_pre_g, ffn1_w_gate, ffn1_w_up, ffn1_w_down, ffn1_post_g, mix_pre_g, w_in, lambda_q1, lambda_k1, lambda_q2, lambda_k2, diff_subln_g, w_attn_proj, w_ret_proj, w_out, mix_post_g, ffn2_pre_g, ffn2_w_gate, ffn2_w_up, ffn2_w_down, ffn2_post_g):
    b, t, d = x.shape
    depth = w_in.shape[0]
    n = b * t
    da_w = DA_HEADS * HEAD_W
    ret_w = RET_HEADS * HEAD_W
    qkv_w = 3 * da_w + 3 * ret_w
    sin, cos, log_g = _retention_tables(t)
    bf = lambda w: w.astype(BF16)

    x2d = x.reshape(n, d)
    for l in range(depth):
        lambda_init = 0.8 - 0.6 * math.exp(-0.3 * l)
        x2d = _ffn(x2d, ffn1_pre_g[l][None], bf(ffn1_w_gate[l]), bf(ffn1_w_up[l]),
                   bf(ffn1_w_down[l]), ffn1_post_g[l][None])

        w = bf(w_in[l])
        qkv, rg, ga, gr = _in_proj(
            x2d, mix_pre_g[l][None], w[:, :qkv_w], w[:, qkv_w:qkv_w + ret_w],
            w[:, qkv_w + ret_w:qkv_w + ret_w + d], w[:, qkv_w + ret_w + d:], q_cols=da_w)
        qkv = qkv.reshape(b, t, qkv_w)
        ya = _diff_attn(qkv, lambda_q1[l][None], lambda_k1[l][None], lambda_q2[l][None],
                        lambda_k2[l][None], diff_subln_g[l][None], lambda_init)
        yr = _retention(qkv, rg.reshape(b, t, ret_w), sin, cos, log_g,
                        q_col=3, k_col=4, v_col=5)
        x2d = _merge(x2d, ya.reshape(n, da_w), yr.reshape(n, ret_w), ga, gr,
                     bf(w_attn_proj[l]), bf(w_ret_proj[l]), bf(w_out[l]), mix_post_g[l][None])

        x2d = _ffn(x2d, ffn2_pre_g[l][None], bf(ffn2_w_gate[l]), bf(ffn2_w_up[l]),
                   bf(ffn2_w_down[l]), ffn2_post_g[l][None])
    return x2d.reshape(b, t, d)
```

```python
import functools
import math

import jax
import jax.numpy as jnp
from jax import lax
from jax.experimental import pallas as pl
from jax.experimental.pallas import tpu as pltpu

F32 = jnp.float32
BF16 = jnp.bfloat16

DA_HEADS = 4
DA_QK = 64
HEAD_W = 128
RET_HEADS = 4
NORM_EPS = 1e-6
SUBLN_EPS = 1e-5
NEG_INF = -1e30
Q_SCALE = DA_QK ** -0.5 * math.log2(math.e)
ONES_ROWS = 16

VMEM_LIMIT_BYTES = 56 * 1024 * 1024

FFN_TOKENS = 512
ROW_GROUPS = 2
ATTN_BLOCK = 256
SCORE_LOOKAHEAD = 2


def _params(*sem):
    return pltpu.CompilerParams(dimension_semantics=sem, vmem_limit_bytes=VMEM_LIMIT_BYTES)


def _resident(shape):
    nd = len(shape)
    return pl.BlockSpec(shape, lambda *_: (0,) * nd, pipeline_mode=pl.Buffered(1))


def _sigmoid(x):
    return 0.5 * jnp.tanh(0.5 * x) + 0.5


def _rmsnorm(x, g, eps):
    return x * lax.rsqrt(jnp.mean(x * x, axis=-1, keepdims=True) + eps) * g


def _ffn_kernel(x_ref, pre_g_ref, wg_ref, wu_ref, wd_ref, post_g_ref, o_ref):
    rows = x_ref.shape[0] // ROW_GROUPS
    for r in range(ROW_GROUPS):
        sl = slice(r * rows, (r + 1) * rows)
        x = x_ref[sl, :]
        h = _rmsnorm(x, pre_g_ref[...], NORM_EPS).astype(BF16)
        g = jnp.dot(h, wg_ref[...], preferred_element_type=F32)
        u = jnp.dot(h, wu_ref[...], preferred_element_type=F32)
        a = (g * _sigmoid(g) * u).astype(BF16)
        y = jnp.dot(a, wd_ref[...], preferred_element_type=F32)
        o_ref[sl, :] = x + 0.5 * _rmsnorm(y, post_g_ref[...], NORM_EPS)


def _ffn(x2d, pre_g, wg, wu, wd, post_g):
    n, d = x2d.shape
    f = wg.shape[1]
    tm = min(FFN_TOKENS, n)
    row = pl.BlockSpec((tm, d), lambda i: (i, 0))
    return pl.pallas_call(
        _ffn_kernel,
        grid=(n // tm,),
        in_specs=[row, _resident((1, d)), _resident((d, f)), _resident((d, f)),
                  _resident((f, d)), _resident((1, d))],
        out_specs=row,
        out_shape=jax.ShapeDtypeStruct((n, d), F32),
        compiler_params=_params("parallel"),
        name="ffn",
    )(x2d, pre_g, wg, wu, wd, post_g)


def _rotate_every_two(x):
    lane = lax.broadcasted_iota(jnp.int32, x.shape, 1)
    nxt = pltpu.roll(x, HEAD_W - 1, 1)
    prv = pltpu.roll(x, 1, 1)
    return jnp.where(lane % 2 == 0, -nxt, prv)


def _retention_stage1(zr, sin, cos, states, xi_ref, zeta_ref, logg_ref, chunk):
    width = RET_HEADS * HEAD_W
    out = []
    for h in range(RET_HEADS):
        q = zr[:, h * HEAD_W:(h + 1) * HEAD_W]
        k = zr[:, width + h * HEAD_W:width + (h + 1) * HEAD_W]
        v = zr[:, 2 * width + h * HEAD_W:2 * width + (h + 1) * HEAD_W].astype(BF16)
        q = q * cos + _rotate_every_two(q) * sin
        k = k * cos + _rotate_every_two(k) * sin
        s = lax.dot_general(q.astype(BF16), k.astype(BF16), (((1,), (1,)), ((), ())),
                            preferred_element_type=F32)
        inter = jnp.dot((q * xi_ref[h]).astype(BF16), states[h].astype(BF16),
                        preferred_element_type=F32)
        kv = lax.dot_general((k * zeta_ref[h]).astype(BF16), v, (((0,), (0,)), ((), ())),
                             preferred_element_type=F32)
        states[h] = jnp.exp(logg_ref[h] * float(chunk)) * states[h] + kv
        out.append((s, inter, v))
    return out


def _retention_stage2(stage1, gate, dmask_ref):
    ys = []
    for h, (s, inter, v) in enumerate(stage1):
        intra = jnp.dot((s * dmask_ref[h]).astype(BF16), v, preferred_element_type=F32)
        y = intra + inter
        y = y * lax.rsqrt(jnp.mean(y * y, axis=-1, keepdims=True) + NORM_EPS)
        g = gate[:, h * HEAD_W:(h + 1) * HEAD_W]
        ys.append(y * (g * _sigmoid(g)))
    return jnp.concatenate(ys, axis=1)


def _mix_proj_kernel(x_ref, g_ref, wr_ref, wa_ref, wga_ref, wgr_ref, sin_ref, cos_ref, logg_ref,
                     qkv_ref, ga_ref, gr_ref, yr_ref,
                     state_ref, dmask_ref, xi_ref, zeta_ref, *, q_cols, tiles_per_seq):
    t = pl.program_id(0)
    chunk = x_ref.shape[0] // ROW_GROUPS
    width = RET_HEADS * HEAD_W
    k_scale = HEAD_W ** -0.5

    @pl.when(t == 0)
    def _():
        state_ref[...] = jnp.zeros(state_ref.shape, F32)
        idx = lax.broadcasted_iota(jnp.int32, (chunk, HEAD_W), 0).astype(F32)
        qi = lax.broadcasted_iota(jnp.int32, (chunk, chunk), 0)
        ki = lax.broadcasted_iota(jnp.int32, (chunk, chunk), 1)
        diff = (qi - ki).astype(F32)
        for h in range(RET_HEADS):
            log_g = logg_ref[h]
            xi_ref[h] = jnp.exp(log_g * (idx + 1.0))
            zeta_ref[h] = jnp.exp(log_g * (chunk - 1.0 - idx)) * k_scale
            dmask_ref[h] = jnp.where(
                diff >= 0, jnp.exp(log_g[:, :1] * jnp.maximum(diff, 0.0)), 0.0) * k_scale

    keep = jnp.where(t % tiles_per_seq == 0, 0.0, 1.0)
    states = [state_ref[h] * keep for h in range(RET_HEADS)]

    groups = [slice(r * chunk, (r + 1) * chunk) for r in range(ROW_GROUPS)]
    hs = [_rmsnorm(x_ref[sl, :], g_ref[...], NORM_EPS).astype(BF16) for sl in groups]
    zrs = [jnp.dot(h, wr_ref[...], preferred_element_type=F32) for h in hs]

    def gate_a(r):
        ga = jnp.dot(hs[r], wga_ref[...], preferred_element_type=F32)
        ga_ref[groups[r], :] = _sigmoid(ga).astype(ga_ref.dtype)

    def gate_r(r):
        gr = jnp.dot(hs[r], wgr_ref[...], preferred_element_type=F32)
        gr_ref[groups[r], :] = _sigmoid(gr).astype(gr_ref.dtype)

    def attn_qkv(r):
        z = jnp.dot(hs[r], wa_ref[...], preferred_element_type=F32)
        qkv_ref[groups[r], :q_cols] = (z[:, :q_cols] * Q_SCALE).astype(BF16)
        qkv_ref[groups[r], q_cols:] = z[:, q_cols:].astype(BF16)

    for r in range(ROW_GROUPS):
        gate_a(r)
        stage1 = _retention_stage1(zrs[r], sin_ref[groups[r], :], cos_ref[groups[r], :], states,
                                   xi_ref, zeta_ref, logg_ref, chunk)
        gate_r(r)
        yr = _retention_stage2(stage1, zrs[r][:, 3 * width:], dmask_ref)
        yr_ref[groups[r], :] = yr.astype(yr_ref.dtype)
        attn_qkv(r)
    for h in range(RET_HEADS):
        state_ref[h] = states[h]


def _mix_proj(x2d, g, wr, wa, wga, wgr, sin, cos, log_g, q_cols, seq_len):
    n, d = x2d.shape
    tm = min(FFN_TOKENS, seq_len)
    chunk = tm // ROW_GROUPS
    tiles_per_seq = seq_len // tm
    ws = (wr, wa, wga, wgr)

    def row(w):
        return pl.BlockSpec((tm, w), lambda i: (i, 0))

    table = pl.BlockSpec((tm, HEAD_W), lambda i: (i % tiles_per_seq, 0))
    outs = (wa.shape[1], wga.shape[1], wgr.shape[1], RET_HEADS * HEAD_W)
    return pl.pallas_call(
        functools.partial(_mix_proj_kernel, q_cols=q_cols, tiles_per_seq=tiles_per_seq),
        grid=(n // tm,),
        in_specs=[row(d), _resident((1, d))] + [_resident(w.shape) for w in ws]
        + [table, table, _resident(log_g.shape)],
        out_specs=[row(w) for w in outs],
        out_shape=[jax.ShapeDtypeStruct((n, w), BF16) for w in outs],
        scratch_shapes=[pltpu.VMEM((RET_HEADS, HEAD_W, HEAD_W), F32),
                        pltpu.VMEM((RET_HEADS, chunk, chunk), F32),
                        pltpu.VMEM((RET_HEADS, chunk, HEAD_W), F32),
                        pltpu.VMEM((RET_HEADS, chunk, HEAD_W), F32)],
        compiler_params=_params("arbitrary"),
        name="mix_proj",
    )(x2d, g, *ws, sin, cos, log_g)


def _diff_attn_kernel(q_ref, k_ref, v_ref, lq1_ref, lk1_ref, lq2_ref, lk2_ref, g_ref,
                      o_ref, qt_ref, vt_ref, m_ref, acc_ref, s_ref, *, blk, lambda_init):
    i = pl.program_id(1)
    n_blocks = v_ref.shape[0] // blk
    heads = range(DA_HEADS)

    @pl.when(i == 0)
    def _():
        for h in heads:
            for c in range(n_blocks):
                vt = v_ref[c * blk:(c + 1) * blk, h * HEAD_W:(h + 1) * HEAD_W].astype(F32).T
                vt_ref[h, c, :HEAD_W, :] = vt.astype(BF16)
                vt_ref[h, c, HEAD_W:, :] = jnp.ones((ONES_ROWS, blk), BF16)

    for h in heads:
        qt = q_ref[:, h * HEAD_W:(h + 1) * HEAD_W].astype(F32).T
        feat = lax.broadcasted_iota(jnp.int32, qt.shape, 0)
        qt_ref[h] = jnp.concatenate([jnp.where(feat < DA_QK, qt, 0.0),
                                     jnp.where(feat >= DA_QK, qt, 0.0)], axis=1).astype(BF16)
    m_ref[...] = jnp.full(m_ref.shape, NEG_INF, F32)
    acc_ref[...] = jnp.zeros(acc_ref.shape, F32)

    def scores(j, h):
        start = pl.multiple_of(j * blk, blk)
        k = k_ref[pl.ds(start, blk), h * HEAD_W:(h + 1) * HEAD_W]
        return jnp.dot(k, qt_ref[h], preferred_element_type=F32)

    def step(j, masked):
        ahead = {}
        for h in heads:
            t = h + SCORE_LOOKAHEAD
            if t < DA_HEADS:
                ahead[t] = scores(j, t)
            elif not masked:
                s_ref[t - DA_HEADS] = scores(j + 1, t - DA_HEADS)
            s = s_ref[h] if h < SCORE_LOOKAHEAD else ahead.pop(h)
            if masked:
                key = lax.broadcasted_iota(jnp.int32, s.shape, 0)
                qry = lax.broadcasted_iota(jnp.int32, s.shape, 1)
                qry = jnp.where(qry >= blk, qry - blk, qry)
                s = jnp.where(key <= qry, s, NEG_INF)
            m_old = m_ref[h]
            m_new = jnp.maximum(m_old, jnp.max(s, axis=0, keepdims=True))
            alpha = jnp.exp2(m_old - m_new)
            p = jnp.exp2(s - m_new).astype(BF16)
            pv = jnp.dot(vt_ref[h, j], p, preferred_element_type=F32)
            acc_ref[h] = alpha * acc_ref[h] + pv
            m_ref[h] = m_new

    def body(j, carry):
        step(j, False)
        return carry

    for h in range(SCORE_LOOKAHEAD):
        s_ref[h] = scores(0, h)
    lax.fori_loop(0, i, body, 0)
    step(i, True)

    lam = (jnp.exp(jnp.sum(lq1_ref[...] * lk1_ref[...], axis=-1, keepdims=True))
           - jnp.exp(jnp.sum(lq2_ref[...] * lk2_ref[...], axis=-1, keepdims=True))
           + lambda_init)
    for h in heads:
        acc = acc_ref[h]
        ot = acc[:HEAD_W] / acc[HEAD_W:HEAD_W + 1]
        ot = ot[:, :blk] - lam * ot[:, blk:]
        ot = ot * lax.rsqrt(jnp.mean(ot * ot, axis=0, keepdims=True) + SUBLN_EPS)
        o_ref[:, h * HEAD_W:(h + 1) * HEAD_W] = (
            ot.T * g_ref[...] * (1.0 - lambda_init)).astype(o_ref.dtype)


def _diff_attn(qkv, lq1, lk1, lq2, lk2, subln_g, lambda_init):
    b, t, _ = qkv.shape
    blk = min(ATTN_BLOCK, t)
    width = DA_HEADS * HEAD_W
    small = _resident((1, DA_QK))
    return pl.pallas_call(
        functools.partial(_diff_attn_kernel, blk=blk, lambda_init=lambda_init),
        grid=(b, t // blk),
        in_specs=[
            pl.BlockSpec((None, blk, width), lambda bi, i: (bi, i, 0)),
            pl.BlockSpec((None, t, width), lambda bi, i: (bi, 0, 1)),
            pl.BlockSpec((None, t, width), lambda bi, i: (bi, 0, 2)),
            small, small, small, small, _resident((1, HEAD_W)),
        ],
        out_specs=pl.BlockSpec((None, blk, width), lambda bi, i: (bi, i, 0)),
        out_shape=jax.ShapeDtypeStruct((b, t, width), BF16),
        scratch_shapes=[
            pltpu.VMEM((DA_HEADS, HEAD_W, 2 * blk), BF16),
            pltpu.VMEM((DA_HEADS, t // blk, HEAD_W + ONES_ROWS, blk), BF16),
            pltpu.VMEM((DA_HEADS, 1, 2 * blk), F32),
            pltpu.VMEM((DA_HEADS, HEAD_W + ONES_ROWS, 2 * blk), F32),
            pltpu.VMEM((SCORE_LOOKAHEAD, blk, 2 * blk), F32),
        ],
        compiler_params=_params("parallel", "arbitrary"),
        name="diff_attn",
    )(qkv, qkv, qkv, lq1, lk1, lq2, lk2, subln_g)


def _merge_kernel(x_ref, ya_ref, yr_ref, ga_ref, gr_ref, wa_ref, wr_ref, wo_ref, g_ref, o_ref):
    ya = jnp.dot(ya_ref[...], wa_ref[...], preferred_element_type=F32)
    yr = jnp.dot(yr_ref[...], wr_ref[...], preferred_element_type=F32)
    merged = ga_ref[...].astype(F32) * ya + gr_ref[...].astype(F32) * yr
    y = jnp.dot(merged.astype(BF16), wo_ref[...], preferred_element_type=F32)
    o_ref[...] = x_ref[...] + _rmsnorm(y, g_ref[...], NORM_EPS)


def _merge(x2d, ya, yr, ga, gr, wa, wr, wo, post_g):
    n, d = x2d.shape
    tm = min(FFN_TOKENS, n)

    def row(w):
        return pl.BlockSpec((tm, w), lambda i: (i, 0))

    return pl.pallas_call(
        _merge_kernel,
        grid=(n // tm,),
        in_specs=[row(d), row(ya.shape[1]), row(yr.shape[1]), row(d), row(d),
                  _resident(wa.shape), _resident(wr.shape), _resident(wo.shape), _resident((1, d))],
        out_specs=row(d),
        out_shape=jax.ShapeDtypeStruct((n, d), F32),
        compiler_params=_params("parallel"),
        name="merge",
    )(x2d, ya, yr, ga, gr, wa, wr, wo, post_g)


def _retention_tables(t):
    angle = 1.0 / (10000.0 ** jnp.linspace(0.0, 1.0, HEAD_W // 2, dtype=F32))
    angle = jnp.repeat(angle, 2)
    ang = jnp.arange(t, dtype=F32)[:, None] * angle[None, :]
    log_g = jnp.log(1.0 - 2.0 ** (-5.0 - jnp.arange(RET_HEADS, dtype=F32)))
    log_g = jnp.broadcast_to(log_g[:, None, None], (RET_HEADS, 1, HEAD_W))
    return jnp.sin(ang), jnp.cos(ang), log_g


def kernel(x, ffn1_pre_g, ffn1_w_gate, ffn1_w_up, ffn1_w_down, ffn1_post_g, mix_pre_g, w_in, lambda_q1, lambda_k1, lambda_q2, lambda_k2, diff_subln_g, w_attn_proj, w_ret_proj, w_out, mix_post_g, ffn2_pre_g, ffn2_w_gate, ffn2_w_up, ffn2_w_down, ffn2_post_g):
    b, t, d = x.shape
    depth = w_in.shape[0]
    n = b * t
    da_w = DA_HEADS * HEAD_W
    att_w = 3 * da_w
    ret_w = 4 * RET_HEADS * HEAD_W
    sin, cos, log_g = _retention_tables(t)
    bf = lambda w: w.astype(BF16)

    x2d = x.reshape(n, d)
    for l in range(depth):
        lambda_init = 0.8 - 0.6 * math.exp(-0.3 * l)
        x2d = _ffn(x2d, ffn1_pre_g[l][None], bf(ffn1_w_gate[l]), bf(ffn1_w_up[l]),
                   bf(ffn1_w_down[l]), ffn1_post_g[l][None])

        w = bf(w_in[l])
        qkv, ga, gr, yr = _mix_proj(
            x2d, mix_pre_g[l][None], w[:, att_w:att_w + ret_w], w[:, :att_w],
            w[:, att_w + ret_w:att_w + ret_w + d], w[:, att_w + ret_w + d:],
            sin, cos, log_g, q_cols=da_w, seq_len=t)
        ya = _diff_attn(qkv.reshape(b, t, att_w), lambda_q1[l][None], lambda_k1[l][None],
                        lambda_q2[l][None], lambda_k2[l][None], diff_subln_g[l][None], lambda_init)
        x2d = _merge(x2d, ya.reshape(n, da_w), yr, ga, gr,
                     bf(w_attn_proj[l]), bf(w_ret_proj[l]), bf(w_out[l]), mix_post_g[l][None])

        x2d = _ffn(x2d, ffn2_pre_g[l][None], bf(ffn2_w_gate[l]), bf(ffn2_w_up[l]),
                   bf(ffn2_w_down[l]), ffn2_post_g[l][None])
    return x2d.reshape(b, t, d)
```

```python
import functools
import math

import jax
import jax.numpy as jnp
from jax import lax
from jax.experimental import pallas as pl
from jax.experimental.pallas import tpu as pltpu

F32 = jnp.float32
BF16 = jnp.bfloat16

DA_HEADS = 4
DA_QK = 64
HEAD_W = 128
RET_HEADS = 4
NORM_EPS = 1e-6
SUBLN_EPS = 1e-5
NEG_INF = -1e30
Q_SCALE = DA_QK ** -0.5 * math.log2(math.e)
ONES_ROWS = 16

VMEM_LIMIT_BYTES = 56 * 1024 * 1024

FFN_TOKENS = 512
ROW_GROUPS = 2
ATTN_BLOCK = 256
SCORE_LOOKAHEAD = 2


def _params(*sem):
    return pltpu.CompilerParams(dimension_semantics=sem, vmem_limit_bytes=VMEM_LIMIT_BYTES)


def _resident(shape):
    nd = len(shape)
    return pl.BlockSpec(shape, lambda *_: (0,) * nd, pipeline_mode=pl.Buffered(1))


def _sigmoid(x):
    return 0.5 * jnp.tanh(0.5 * x) + 0.5


def _rmsnorm(x, g, eps):
    return x * lax.rsqrt(jnp.mean(x * x, axis=-1, keepdims=True) + eps) * g


def _ffn_kernel(x_ref, pre_g_ref, wg_ref, wu_ref, wd_ref, post_g_ref, o_ref):
    rows = x_ref.shape[0] // ROW_GROUPS
    for r in range(ROW_GROUPS):
        sl = slice(r * rows, (r + 1) * rows)
        x = x_ref[sl, :]
        h = _rmsnorm(x, pre_g_ref[...], NORM_EPS).astype(BF16)
        g = jnp.dot(h, wg_ref[...], preferred_element_type=F32)
        u = jnp.dot(h, wu_ref[...], preferred_element_type=F32)
        a = (g * _sigmoid(g) * u).astype(BF16)
        y = jnp.dot(a, wd_ref[...], preferred_element_type=F32)
        o_ref[sl, :] = x + 0.5 * _rmsnorm(y, post_g_ref[...], NORM_EPS)


def _ffn(x2d, pre_g, wg, wu, wd, post_g):
    n, d = x2d.shape
    f = wg.shape[1]
    tm = min(FFN_TOKENS, n)
    row = pl.BlockSpec((tm, d), lambda i: (i, 0))
    return pl.pallas_call(
        _ffn_kernel,
        grid=(n // tm,),
        in_specs=[row, _resident((1, d)), _resident((d, f)), _resident((d, f)),
                  _resident((f, d)), _resident((1, d))],
        out_specs=row,
        out_shape=jax.ShapeDtypeStruct((n, d), F32),
        compiler_params=_params("parallel"),
        name="ffn",
    )(x2d, pre_g, wg, wu, wd, post_g)


def _rotate_every_two(x):
    lane = lax.broadcasted_iota(jnp.int32, x.shape, 1)
    nxt = pltpu.roll(x, HEAD_W - 1, 1)
    prv = pltpu.roll(x, 1, 1)
    return jnp.where(lane % 2 == 0, -nxt, prv)


def _retention_stage1(zr, sin, cos, states, xi_ref, zeta_ref, logg_ref, chunk):
    width = RET_HEADS * HEAD_W
    out = []
    for h in range(RET_HEADS):
        q = zr[:, h * HEAD_W:(h + 1) * HEAD_W]
        k = zr[:, width + h * HEAD_W:width + (h + 1) * HEAD_W]
        v = zr[:, 2 * width + h * HEAD_W:2 * width + (h + 1) * HEAD_W].astype(BF16)
        q = q * cos + _rotate_every_two(q) * sin
        k = k * cos + _rotate_every_two(k) * sin
        s = lax.dot_general(q.astype(BF16), k.astype(BF16), (((1,), (1,)), ((), ())),
                            preferred_element_type=F32)
        inter = jnp.dot((q * xi_ref[h]).astype(BF16), states[h].astype(BF16),
                        preferred_element_type=F32)
        kv = lax.dot_general((k * zeta_ref[h]).astype(BF16), v, (((0,), (0,)), ((), ())),
                             preferred_element_type=F32)
        states[h] = jnp.exp(logg_ref[h] * float(chunk)) * states[h] + kv
        out.append((s, inter, v))
    return out


def _retention_stage2(stage1, gate, dmask_ref):
    ys = []
    for h, (s, inter, v) in enumerate(stage1):
        intra = jnp.dot((s * dmask_ref[h]).astype(BF16), v, preferred_element_type=F32)
        y = intra + inter
        y = y * lax.rsqrt(jnp.mean(y * y, axis=-1, keepdims=True) + NORM_EPS)
        g = gate[:, h * HEAD_W:(h + 1) * HEAD_W]
        ys.append(y * (g * _sigmoid(g)))
    return jnp.concatenate(ys, axis=1)


def _mix_proj_kernel(x_ref, g_ref, wr_ref, wa_ref, wga_ref, wgr_ref, sin_ref, cos_ref, logg_ref,
                     qt_ref, k_ref, vt_ref, ga_ref, gr_ref, yr_ref,
                     state_ref, dmask_ref, xi_ref, zeta_ref, *, tiles_per_seq):
    t = pl.program_id(0)
    chunk = x_ref.shape[0] // ROW_GROUPS
    width = RET_HEADS * HEAD_W
    k_scale = HEAD_W ** -0.5

    @pl.when(t == 0)
    def _():
        state_ref[...] = jnp.zeros(state_ref.shape, F32)
        idx = lax.broadcasted_iota(jnp.int32, (chunk, HEAD_W), 0).astype(F32)
        qi = lax.broadcasted_iota(jnp.int32, (chunk, chunk), 0)
        ki = lax.broadcasted_iota(jnp.int32, (chunk, chunk), 1)
        diff = (qi - ki).astype(F32)
        for h in range(RET_HEADS):
            log_g = logg_ref[h]
            xi_ref[h] = jnp.exp(log_g * (idx + 1.0))
            zeta_ref[h] = jnp.exp(log_g * (chunk - 1.0 - idx)) * k_scale
            dmask_ref[h] = jnp.where(
                diff >= 0, jnp.exp(log_g[:, :1] * jnp.maximum(diff, 0.0)), 0.0) * k_scale

    keep = jnp.where(t % tiles_per_seq == 0, 0.0, 1.0)
    states = [state_ref[h] * keep for h in range(RET_HEADS)]

    groups = [slice(r * chunk, (r + 1) * chunk) for r in range(ROW_GROUPS)]
    hs = [_rmsnorm(x_ref[sl, :], g_ref[...], NORM_EPS).astype(BF16) for sl in groups]
    zrs = [jnp.dot(h, wr_ref[...], preferred_element_type=F32) for h in hs]

    def gate_a(r):
        ga = jnp.dot(hs[r], wga_ref[...], preferred_element_type=F32)
        ga_ref[groups[r], :] = _sigmoid(ga).astype(ga_ref.dtype)

    def gate_r(r):
        gr = jnp.dot(hs[r], wgr_ref[...], preferred_element_type=F32)
        gr_ref[groups[r], :] = _sigmoid(gr).astype(gr_ref.dtype)

    def attn_qkv(r):
        da_w = DA_HEADS * HEAD_W
        z = jnp.dot(hs[r], wa_ref[...], preferred_element_type=F32)
        k_ref[groups[r], :] = z[:, da_w:2 * da_w].astype(BF16)
        feat = lax.broadcasted_iota(jnp.int32, (HEAD_W, chunk), 0)
        for h in range(DA_HEADS):
            qt = (z[:, h * HEAD_W:(h + 1) * HEAD_W] * Q_SCALE).T
            qt_ref[r, h] = jnp.concatenate([jnp.where(feat < DA_QK, qt, 0.0),
                                            jnp.where(feat >= DA_QK, qt, 0.0)], axis=1).astype(BF16)
            vt = z[:, 2 * da_w + h * HEAD_W:2 * da_w + (h + 1) * HEAD_W].T
            vt_ref[r, h, :HEAD_W, :] = vt.astype(BF16)
            vt_ref[r, h, HEAD_W:, :] = jnp.ones((ONES_ROWS, chunk), BF16)

    for r in range(ROW_GROUPS):
        gate_a(r)
        stage1 = _retention_stage1(zrs[r], sin_ref[groups[r], :], cos_ref[groups[r], :], states,
                                   xi_ref, zeta_ref, logg_ref, chunk)
        gate_r(r)
        yr = _retention_stage2(stage1, zrs[r][:, 3 * width:], dmask_ref)
        yr_ref[groups[r], :] = yr.astype(yr_ref.dtype)
        attn_qkv(r)
    for h in range(RET_HEADS):
        state_ref[h] = states[h]


def _mix_proj(x2d, g, wr, wa, wga, wgr, sin, cos, log_g, seq_len):
    n, d = x2d.shape
    tm = min(FFN_TOKENS, seq_len)
    chunk = tm // ROW_GROUPS
    tiles_per_seq = seq_len // tm
    ws = (wr, wa, wga, wgr)

    def row(w):
        return pl.BlockSpec((tm, w), lambda i: (i, 0))

    table = pl.BlockSpec((tm, HEAD_W), lambda i: (i % tiles_per_seq, 0))
    assert chunk == min(ATTN_BLOCK, seq_len)
    rows = (DA_HEADS * HEAD_W, wga.shape[1], wgr.shape[1], RET_HEADS * HEAD_W)
    qt_shape = (n // chunk, DA_HEADS, HEAD_W, 2 * chunk)
    vt_shape = (n // chunk, DA_HEADS, HEAD_W + ONES_ROWS, chunk)

    def blocks(shape):
        return pl.BlockSpec((ROW_GROUPS,) + shape[1:], lambda i: (i, 0, 0, 0))

    return pl.pallas_call(
        functools.partial(_mix_proj_kernel, tiles_per_seq=tiles_per_seq),
        grid=(n // tm,),
        in_specs=[row(d), _resident((1, d))] + [_resident(w.shape) for w in ws]
        + [table, table, _resident(log_g.shape)],
        out_specs=[blocks(qt_shape), row(rows[0]), blocks(vt_shape)] + [row(w) for w in rows[1:]],
        out_shape=[jax.ShapeDtypeStruct(qt_shape, BF16), jax.ShapeDtypeStruct((n, rows[0]), BF16),
                   jax.ShapeDtypeStruct(vt_shape, BF16)]
        + [jax.ShapeDtypeStruct((n, w), BF16) for w in rows[1:]],
        scratch_shapes=[pltpu.VMEM((RET_HEADS, HEAD_W, HEAD_W), F32),
                        pltpu.VMEM((RET_HEADS, chunk, chunk), F32),
                        pltpu.VMEM((RET_HEADS, chunk, HEAD_W), F32),
                        pltpu.VMEM((RET_HEADS, chunk, HEAD_W), F32)],
        compiler_params=_params("arbitrary"),
        name="mix_proj",
    )(x2d, g, *ws, sin, cos, log_g)


def _diff_attn_kernel(qt_ref, k_ref, vt_ref, lq1_ref, lk1_ref, lq2_ref, lk2_ref, g_ref,
                      o_ref, m_ref, acc_ref, s_ref, *, blk, lambda_init):
    i = pl.program_id(1)
    heads = range(DA_HEADS)

    m_ref[...] = jnp.full(m_ref.shape, NEG_INF, F32)
    acc_ref[...] = jnp.zeros(acc_ref.shape, F32)

    def scores(j, h):
        start = pl.multiple_of(j * blk, blk)
        k = k_ref[pl.ds(start, blk), h * HEAD_W:(h + 1) * HEAD_W]
        return jnp.dot(k, qt_ref[h], preferred_element_type=F32)

    def step(j, masked):
        ahead = {}
        for h in heads:
            t = h + SCORE_LOOKAHEAD
            if t < DA_HEADS:
                ahead[t] = scores(j, t)
            elif not masked:
                s_ref[t - DA_HEADS] = scores(j + 1, t - DA_HEADS)
            s = s_ref[h] if h < SCORE_LOOKAHEAD else ahead.pop(h)
            if masked:
                key = lax.broadcasted_iota(jnp.int32, s.shape, 0)
                qry = lax.broadcasted_iota(jnp.int32, s.shape, 1)
                qry = jnp.where(qry >= blk, qry - blk, qry)
                s = jnp.where(key <= qry, s, NEG_INF)
            m_old = m_ref[h]
            m_new = jnp.maximum(m_old, jnp.max(s, axis=0, keepdims=True))
            alpha = jnp.exp2(m_old - m_new)
            p = jnp.exp2(s - m_new).astype(BF16)
            pv = jnp.dot(vt_ref[j, h], p, preferred_element_type=F32)
            acc_ref[h] = alpha * acc_ref[h] + pv
            m_ref[h] = m_new

    def body(j, carry):
        step(j, False)
        return carry

    for h in range(SCORE_LOOKAHEAD):
        s_ref[h] = scores(0, h)
    lax.fori_loop(0, i, body, 0)
    step(i, True)

    lam = (jnp.exp(jnp.sum(lq1_ref[...] * lk1_ref[...], axis=-1, keepdims=True))
           - jnp.exp(jnp.sum(lq2_ref[...] * lk2_ref[...], axis=-1, keepdims=True))
           + lambda_init)
    for h in heads:
        acc = acc_ref[h]
        ot = acc[:HEAD_W] / acc[HEAD_W:HEAD_W + 1]
        ot = ot[:, :blk] - lam * ot[:, blk:]
        ot = ot * lax.rsqrt(jnp.mean(ot * ot, axis=0, keepdims=True) + SUBLN_EPS)
        o_ref[:, h * HEAD_W:(h + 1) * HEAD_W] = (
            ot.T * g_ref[...] * (1.0 - lambda_init)).astype(o_ref.dtype)


def _diff_attn(qt, k, vt, lq1, lk1, lq2, lk2, subln_g, lambda_init):
    b, t, width = k.shape
    blk = vt.shape[-1]
    small = _resident((1, DA_QK))
    return pl.pallas_call(
        functools.partial(_diff_attn_kernel, blk=blk, lambda_init=lambda_init),
        grid=(b, t // blk),
        in_specs=[
            pl.BlockSpec((None, None) + qt.shape[2:], lambda bi, i: (bi, i, 0, 0, 0)),
            pl.BlockSpec((None, t, width), lambda bi, i: (bi, 0, 0)),
            pl.BlockSpec((None,) + vt.shape[1:], lambda bi, i: (bi, 0, 0, 0, 0)),
            small, small, small, small, _resident((1, HEAD_W)),
        ],
        out_specs=pl.BlockSpec((None, blk, width), lambda bi, i: (bi, i, 0)),
        out_shape=jax.ShapeDtypeStruct((b, t, width), BF16),
        scratch_shapes=[
            pltpu.VMEM((DA_HEADS, 1, 2 * blk), F32),
            pltpu.VMEM((DA_HEADS, HEAD_W + ONES_ROWS, 2 * blk), F32),
            pltpu.VMEM((SCORE_LOOKAHEAD, blk, 2 * blk), F32),
        ],
        compiler_params=_params("parallel", "arbitrary"),
        name="diff_attn",
    )(qt, k, vt, lq1, lk1, lq2, lk2, subln_g)


def _merge_kernel(x_ref, ya_ref, yr_ref, ga_ref, gr_ref, wa_ref, wr_ref, wo_ref, g_ref, o_ref):
    ya = jnp.dot(ya_ref[...], wa_ref[...], preferred_element_type=F32)
    yr = jnp.dot(yr_ref[...], wr_ref[...], preferred_element_type=F32)
    merged = ga_ref[...].astype(F32) * ya + gr_ref[...].astype(F32) * yr
    y = jnp.dot(merged.astype(BF16), wo_ref[...], preferred_element_type=F32)
    o_ref[...] = x_ref[...] + _rmsnorm(y, g_ref[...], NORM_EPS)


def _merge(x2d, ya, yr, ga, gr, wa, wr, wo, post_g):
    n, d = x2d.shape
    tm = min(FFN_TOKENS, n)

    def row(w):
        return pl.BlockSpec((tm, w), lambda i: (i, 0))

    return pl.pallas_call(
        _merge_kernel,
        grid=(n // tm,),
        in_specs=[row(d), row(ya.shape[1]), row(yr.shape[1]), row(d), row(d),
                  _resident(wa.shape), _resident(wr.shape), _resident(wo.shape), _resident((1, d))],
        out_specs=row(d),
        out_shape=jax.ShapeDtypeStruct((n, d), F32),
        compiler_params=_params("parallel"),
        name="merge",
    )(x2d, ya, yr, ga, gr, wa, wr, wo, post_g)


def _retention_tables(t):
    angle = 1.0 / (10000.0 ** jnp.linspace(0.0, 1.0, HEAD_W // 2, dtype=F32))
    angle = jnp.repeat(angle, 2)
    ang = jnp.arange(t, dtype=F32)[:, None] * angle[None, :]
    log_g = jnp.log(1.0 - 2.0 ** (-5.0 - jnp.arange(RET_HEADS, dtype=F32)))
    log_g = jnp.broadcast_to(log_g[:, None, None], (RET_HEADS, 1, HEAD_W))
    return jnp.sin(ang), jnp.cos(ang), log_g


def kernel(x, ffn1_pre_g, ffn1_w_gate, ffn1_w_up, ffn1_w_down, ffn1_post_g, mix_pre_g, w_in, lambda_q1, lambda_k1, lambda_q2, lambda_k2, diff_subln_g, w_attn_proj, w_ret_proj, w_out, mix_post_g, ffn2_pre_g, ffn2_w_gate, ffn2_w_up, ffn2_w_down, ffn2_post_g):
    b, t, d = x.shape
    depth = w_in.shape[0]
    n = b * t
    da_w = DA_HEADS * HEAD_W
    att_w = 3 * da_w
    ret_w = 4 * RET_HEADS * HEAD_W
    sin, cos, log_g = _retention_tables(t)
    bf = lambda w: w.astype(BF16)

    x2d = x.reshape(n, d)
    for l in range(depth):
        lambda_init = 0.8 - 0.6 * math.exp(-0.3 * l)
        x2d = _ffn(x2d, ffn1_pre_g[l][None], bf(ffn1_w_gate[l]), bf(ffn1_w_up[l]),
                   bf(ffn1_w_down[l]), ffn1_post_g[l][None])

        w = bf(w_in[l])
        qt, k, vt, ga, gr, yr = _mix_proj(
            x2d, mix_pre_g[l][None], w[:, att_w:att_w + ret_w], w[:, :att_w],
            w[:, att_w + ret_w:att_w + ret_w + d], w[:, att_w + ret_w + d:],
            sin, cos, log_g, seq_len=t)
        ya = _diff_attn(qt.reshape((b, -1) + qt.shape[1:]), k.reshape(b, t, da_w),
                        vt.reshape((b, -1) + vt.shape[1:]), lambda_q1[l][None], lambda_k1[l][None],
                        lambda_q2[l][None], lambda_k2[l][None], diff_subln_g[l][None], lambda_init)
        x2d = _merge(x2d, ya.reshape(n, da_w), yr, ga, gr,
                     bf(w_attn_proj[l]), bf(w_ret_proj[l]), bf(w_out[l]), mix_post_g[l][None])

        x2d = _ffn(x2d, ffn2_pre_g[l][None], bf(ffn2_w_gate[l]), bf(ffn2_w_up[l]),
                   bf(ffn2_w_down[l]), ffn2_post_g[l][None])
    return x2d.reshape(b, t, d)
```

```python
import functools
import math

import jax
import jax.numpy as jnp
from jax import lax
from jax.experimental import pallas as pl
from jax.experimental.pallas import tpu as pltpu

F32 = jnp.float32
BF16 = jnp.bfloat16

DA_HEADS = 4
DA_QK = 64
HEAD_W = 128
RET_HEADS = 4
NORM_EPS = 1e-6
SUBLN_EPS = 1e-5
NEG_INF = -1e30
Q_SCALE = DA_QK ** -0.5 * math.log2(math.e)
ONES_ROWS = 16

VMEM_LIMIT_BYTES = 56 * 1024 * 1024

FFN_TOKENS = 512
ROW_GROUPS = 2
ATTN_BLOCK = 256
SCORE_LOOKAHEAD = 2


def _params(*sem):
    return pltpu.CompilerParams(dimension_semantics=sem, vmem_limit_bytes=VMEM_LIMIT_BYTES)


def _resident(shape):
    nd = len(shape)
    return pl.BlockSpec(shape, lambda *_: (0,) * nd, pipeline_mode=pl.Buffered(1))


def _sigmoid(x):
    return 0.5 * jnp.tanh(0.5 * x) + 0.5


def _rmsnorm(x, g, eps):
    return x * lax.rsqrt(jnp.mean(x * x, axis=-1, keepdims=True) + eps) * g


def _ffn_rows(x, pre_g_ref, wg_ref, wu_ref, wd_ref, post_g_ref):
    h = _rmsnorm(x, pre_g_ref[...], NORM_EPS).astype(BF16)
    g = jnp.dot(h, wg_ref[...], preferred_element_type=F32)
    u = jnp.dot(h, wu_ref[...], preferred_element_type=F32)
    a = (g * _sigmoid(g) * u).astype(BF16)
    y = jnp.dot(a, wd_ref[...], preferred_element_type=F32)
    return x + 0.5 * _rmsnorm(y, post_g_ref[...], NORM_EPS)


def _rotate_every_two(x):
    lane = lax.broadcasted_iota(jnp.int32, x.shape, 1)
    nxt = pltpu.roll(x, HEAD_W - 1, 1)
    prv = pltpu.roll(x, 1, 1)
    return jnp.where(lane % 2 == 0, -nxt, prv)


def _retention_stage1(zr, sin, cos, states, xi_ref, zeta_ref, logg_ref, chunk):
    width = RET_HEADS * HEAD_W
    out = []
    for h in range(RET_HEADS):
        q = zr[:, h * HEAD_W:(h + 1) * HEAD_W]
        k = zr[:, width + h * HEAD_W:width + (h + 1) * HEAD_W]
        v = zr[:, 2 * width + h * HEAD_W:2 * width + (h + 1) * HEAD_W].astype(BF16)
        q = q * cos + _rotate_every_two(q) * sin
        k = k * cos + _rotate_every_two(k) * sin
        s = lax.dot_general(q.astype(BF16), k.astype(BF16), (((1,), (1,)), ((), ())),
                            preferred_element_type=F32)
        inter = jnp.dot((q * xi_ref[h]).astype(BF16), states[h].astype(BF16),
                        preferred_element_type=F32)
        kv = lax.dot_general((k * zeta_ref[h]).astype(BF16), v, (((0,), (0,)), ((), ())),
                             preferred_element_type=F32)
        states[h] = jnp.exp(logg_ref[h] * float(chunk)) * states[h] + kv
        out.append((s, inter, v))
    return out


def _retention_stage2(stage1, gate, dmask_ref):
    ys = []
    for h, (s, inter, v) in enumerate(stage1):
        intra = jnp.dot((s * dmask_ref[h]).astype(BF16), v, preferred_element_type=F32)
        y = intra + inter
        y = y * lax.rsqrt(jnp.mean(y * y, axis=-1, keepdims=True) + NORM_EPS)
        g = gate[:, h * HEAD_W:(h + 1) * HEAD_W]
        ys.append(y * (g * _sigmoid(g)))
    return jnp.concatenate(ys, axis=1)


def _mix_proj_kernel(x_ref, pre_g_ref, wg_ref, wu_ref, wd_ref, post_g_ref,
                     g_ref, wr_ref, wa_ref, wga_ref, wgr_ref, sin_ref, cos_ref, logg_ref,
                     x1_ref, qt_ref, k_ref, vt_ref, ga_ref, gr_ref, yr_ref,
                     state_ref, dmask_ref, xi_ref, zeta_ref, *, tiles_per_seq):
    t = pl.program_id(0)
    chunk = x_ref.shape[0] // ROW_GROUPS
    width = RET_HEADS * HEAD_W
    k_scale = HEAD_W ** -0.5

    @pl.when(t == 0)
    def _():
        state_ref[...] = jnp.zeros(state_ref.shape, F32)
        idx = lax.broadcasted_iota(jnp.int32, (chunk, HEAD_W), 0).astype(F32)
        qi = lax.broadcasted_iota(jnp.int32, (chunk, chunk), 0)
        ki = lax.broadcasted_iota(jnp.int32, (chunk, chunk), 1)
        diff = (qi - ki).astype(F32)
        for h in range(RET_HEADS):
            log_g = logg_ref[h]
            xi_ref[h] = jnp.exp(log_g * (idx + 1.0))
            zeta_ref[h] = jnp.exp(log_g * (chunk - 1.0 - idx)) * k_scale
            dmask_ref[h] = jnp.where(
                diff >= 0, jnp.exp(log_g[:, :1] * jnp.maximum(diff, 0.0)), 0.0) * k_scale

    keep = jnp.where(t % tiles_per_seq == 0, 0.0, 1.0)
    states = [state_ref[h] * keep for h in range(RET_HEADS)]

    groups = [slice(r * chunk, (r + 1) * chunk) for r in range(ROW_GROUPS)]
    hs = []
    for sl in groups:
        x1 = _ffn_rows(x_ref[sl, :], pre_g_ref, wg_ref, wu_ref, wd_ref, post_g_ref)
        x1_ref[sl, :] = x1
        hs.append(_rmsnorm(x1, g_ref[...], NORM_EPS).astype(BF16))
    zrs = [jnp.dot(h, wr_ref[...], preferred_element_type=F32) for h in hs]

    def gate_a(r):
        ga = jnp.dot(hs[r], wga_ref[...], preferred_element_type=F32)
        ga_ref[groups[r], :] = _sigmoid(ga).astype(ga_ref.dtype)

    def gate_r(r):
        gr = jnp.dot(hs[r], wgr_ref[...], preferred_element_type=F32)
        gr_ref[groups[r], :] = _sigmoid(gr).astype(gr_ref.dtype)

    def attn_qkv(r):
        da_w = DA_HEADS * HEAD_W
        z = jnp.dot(hs[r], wa_ref[...], preferred_element_type=F32)
        k_ref[groups[r], :] = z[:, da_w:2 * da_w].astype(BF16)
        feat = lax.broadcasted_iota(jnp.int32, (HEAD_W, chunk), 0)
        for h in range(DA_HEADS):
            qt = (z[:, h * HEAD_W:(h + 1) * HEAD_W] * Q_SCALE).T
            qt_ref[r, h] = jnp.concatenate([jnp.where(feat < DA_QK, qt, 0.0),
                                            jnp.where(feat >= DA_QK, qt, 0.0)], axis=1).astype(BF16)
            vt = z[:, 2 * da_w + h * HEAD_W:2 * da_w + (h + 1) * HEAD_W].T
            vt_ref[r, h, :HEAD_W, :] = vt.astype(BF16)
            vt_ref[r, h, HEAD_W:, :] = jnp.ones((ONES_ROWS, chunk), BF16)

    for r in range(ROW_GROUPS):
        gate_a(r)
        stage1 = _retention_stage1(zrs[r], sin_ref[groups[r], :], cos_ref[groups[r], :], states,
                                   xi_ref, zeta_ref, logg_ref, chunk)
        gate_r(r)
        yr = _retention_stage2(stage1, zrs[r][:, 3 * width:], dmask_ref)
        yr_ref[groups[r], :] = yr.astype(yr_ref.dtype)
        attn_qkv(r)
    for h in range(RET_HEADS):
        state_ref[h] = states[h]


def _mix_proj(x2d, pre_g, wg, wu, wd, post_g, g, wr, wa, wga, wgr, sin, cos, log_g, seq_len):
    n, d = x2d.shape
    tm = min(FFN_TOKENS, seq_len)
    chunk = tm // ROW_GROUPS
    tiles_per_seq = seq_len // tm
    ws = (pre_g, wg, wu, wd, post_g, g, wr, wa, wga, wgr)

    def row(w):
        return pl.BlockSpec((tm, w), lambda i: (i, 0))

    table = pl.BlockSpec((tm, HEAD_W), lambda i: (i % tiles_per_seq, 0))
    assert chunk == min(ATTN_BLOCK, seq_len)
    rows = (DA_HEADS * HEAD_W, wga.shape[1], wgr.shape[1], RET_HEADS * HEAD_W)
    qt_shape = (n // chunk, DA_HEADS, HEAD_W, 2 * chunk)
    vt_shape = (n // chunk, DA_HEADS, HEAD_W + ONES_ROWS, chunk)

    def blocks(shape):
        return pl.BlockSpec((ROW_GROUPS,) + shape[1:], lambda i: (i, 0, 0, 0))

    return pl.pallas_call(
        functools.partial(_mix_proj_kernel, tiles_per_seq=tiles_per_seq),
        grid=(n // tm,),
        in_specs=[row(d)] + [_resident(w.shape) for w in ws]
        + [table, table, _resident(log_g.shape)],
        out_specs=[row(d), blocks(qt_shape), row(rows[0]), blocks(vt_shape)] + [row(w) for w in rows[1:]],
        out_shape=[jax.ShapeDtypeStruct((n, d), F32),
                   jax.ShapeDtypeStruct(qt_shape, BF16), jax.ShapeDtypeStruct((n, rows[0]), BF16),
                   jax.ShapeDtypeStruct(vt_shape, BF16)]
        + [jax.ShapeDtypeStruct((n, w), BF16) for w in rows[1:]],
        scratch_shapes=[pltpu.VMEM((RET_HEADS, HEAD_W, HEAD_W), F32),
                        pltpu.VMEM((RET_HEADS, chunk, chunk), F32),
                        pltpu.VMEM((RET_HEADS, chunk, HEAD_W), F32),
                        pltpu.VMEM((RET_HEADS, chunk, HEAD_W), F32)],
        compiler_params=_params("arbitrary"),
        name="mix_proj",
    )(x2d, *ws, sin, cos, log_g)


def _diff_attn_kernel(qt_ref, k_ref, vt_ref, lq1_ref, lk1_ref, lq2_ref, lk2_ref, g_ref,
                      o_ref, m_ref, acc_ref, s_ref, *, blk, lambda_init):
    i = pl.program_id(1)
    heads = range(DA_HEADS)

    m_ref[...] = jnp.full(m_ref.shape, NEG_INF, F32)
    acc_ref[...] = jnp.zeros(acc_ref.shape, F32)

    def scores(j, h):
        start = pl.multiple_of(j * blk, blk)
        k = k_ref[pl.ds(start, blk), h * HEAD_W:(h + 1) * HEAD_W]
        return jnp.dot(k, qt_ref[h], preferred_element_type=F32)

    def step(j, masked):
        ahead = {}
        for h in heads:
            t = h + SCORE_LOOKAHEAD
            if t < DA_HEADS:
                ahead[t] = scores(j, t)
            elif not masked:
                s_ref[t - DA_HEADS] = scores(j + 1, t - DA_HEADS)
            s = s_ref[h] if h < SCORE_LOOKAHEAD else ahead.pop(h)
            if masked:
                key = lax.broadcasted_iota(jnp.int32, s.shape, 0)
                qry = lax.broadcasted_iota(jnp.int32, s.shape, 1)
                qry = jnp.where(qry >= blk, qry - blk, qry)
                s = jnp.where(key <= qry, s, NEG_INF)
            m_old = m_ref[h]
            m_new = jnp.maximum(m_old, jnp.max(s, axis=0, keepdims=True))
            alpha = jnp.exp2(m_old - m_new)
            p = jnp.exp2(s - m_new).astype(BF16)
            pv = jnp.dot(vt_ref[j, h], p, preferred_element_type=F32)
            acc_ref[h] = alpha * acc_ref[h] + pv
            m_ref[h] = m_new

    def body(j, carry):
        step(j, False)
        return carry

    for h in range(SCORE_LOOKAHEAD):
        s_ref[h] = scores(0, h)
    lax.fori_loop(0, i, body, 0)
    step(i, True)

    lam = (jnp.exp(jnp.sum(lq1_ref[...] * lk1_ref[...], axis=-1, keepdims=True))
           - jnp.exp(jnp.sum(lq2_ref[...] * lk2_ref[...], axis=-1, keepdims=True))
           + lambda_init)
    for h in heads:
        acc = acc_ref[h]
        ot = acc[:HEAD_W] / acc[HEAD_W:HEAD_W + 1]
        ot = ot[:, :blk] - lam * ot[:, blk:]
        ot = ot * lax.rsqrt(jnp.mean(ot * ot, axis=0, keepdims=True) + SUBLN_EPS)
        o_ref[:, h * HEAD_W:(h + 1) * HEAD_W] = (
            ot.T * g_ref[...] * (1.0 - lambda_init)).astype(o_ref.dtype)


def _diff_attn(qt, k, vt, lq1, lk1, lq2, lk2, subln_g, lambda_init):
    b, t, width = k.shape
    blk = vt.shape[-1]
    small = _resident((1, DA_QK))
    return pl.pallas_call(
        functools.partial(_diff_attn_kernel, blk=blk, lambda_init=lambda_init),
        grid=(b, t // blk),
        in_specs=[
            pl.BlockSpec((None, None) + qt.shape[2:], lambda bi, i: (bi, i, 0, 0, 0)),
            pl.BlockSpec((None, t, width), lambda bi, i: (bi, 0, 0)),
            pl.BlockSpec((None,) + vt.shape[1:], lambda bi, i: (bi, 0, 0, 0, 0)),
            small, small, small, small, _resident((1, HEAD_W)),
        ],
        out_specs=pl.BlockSpec((None, blk, width), lambda bi, i: (bi, i, 0)),
        out_shape=jax.ShapeDtypeStruct((b, t, width), BF16),
        scratch_shapes=[
            pltpu.VMEM((DA_HEADS, 1, 2 * blk), F32),
            pltpu.VMEM((DA_HEADS, HEAD_W + ONES_ROWS, 2 * blk), F32),
            pltpu.VMEM((SCORE_LOOKAHEAD, blk, 2 * blk), F32),
        ],
        compiler_params=_params("parallel", "arbitrary"),
        name="diff_attn",
    )(qt, k, vt, lq1, lk1, lq2, lk2, subln_g)


def _merge_ffn_kernel(x_ref, ya_ref, yr_ref, ga_ref, gr_ref, wa_ref, wr_ref, wo_ref, mix_g_ref,
                      pre_g_ref, wg_ref, wu_ref, wd_ref, post_g_ref, o_ref):
    rows = x_ref.shape[0] // ROW_GROUPS
    groups = [slice(r * rows, (r + 1) * rows) for r in range(ROW_GROUPS)]
    xs = []
    for sl in groups:
        ya = jnp.dot(ya_ref[sl, :], wa_ref[...], preferred_element_type=F32)
        yr = jnp.dot(yr_ref[sl, :], wr_ref[...], preferred_element_type=F32)
        merged = ga_ref[sl, :].astype(F32) * ya + gr_ref[sl, :].astype(F32) * yr
        y = jnp.dot(merged.astype(BF16), wo_ref[...], preferred_element_type=F32)
        xs.append(x_ref[sl, :] + _rmsnorm(y, mix_g_ref[...], NORM_EPS))
    for sl, x in zip(groups, xs):
        o_ref[sl, :] = _ffn_rows(x, pre_g_ref, wg_ref, wu_ref, wd_ref, post_g_ref)


def _merge_ffn(x2d, ya, yr, ga, gr, wa, wr, wo, mix_g, pre_g, wg, wu, wd, post_g):
    n, d = x2d.shape
    tm = min(FFN_TOKENS, n)

    def row(w):
        return pl.BlockSpec((tm, w), lambda i: (i, 0))

    weights = (wa, wr, wo, mix_g, pre_g, wg, wu, wd, post_g)
    return pl.pallas_call(
        _merge_ffn_kernel,
        grid=(n // tm,),
        in_specs=[row(d), row(ya.shape[1]), row(yr.shape[1]), row(d), row(d)]
        + [_resident(w.shape) for w in weights],
        out_specs=row(d),
        out_shape=jax.ShapeDtypeStruct((n, d), F32),
        compiler_params=_params("parallel"),
        name="merge_ffn",
    )(x2d, ya, yr, ga, gr, *weights)


def _retention_tables(t):
    angle = 1.0 / (10000.0 ** jnp.linspace(0.0, 1.0, HEAD_W // 2, dtype=F32))
    angle = jnp.repeat(angle, 2)
    ang = jnp.arange(t, dtype=F32)[:, None] * angle[None, :]
    log_g = jnp.log(1.0 - 2.0 ** (-5.0 - jnp.arange(RET_HEADS, dtype=F32)))
    log_g = jnp.broadcast_to(log_g[:, None, None], (RET_HEADS, 1, HEAD_W))
    return jnp.sin(ang), jnp.cos(ang), log_g


def kernel(x, ffn1_pre_g, ffn1_w_gate, ffn1_w_up, ffn1_w_down, ffn1_post_g, mix_pre_g, w_in, lambda_q1, lambda_k1, lambda_q2, lambda_k2, diff_subln_g, w_attn_proj, w_ret_proj, w_out, mix_post_g, ffn2_pre_g, ffn2_w_gate, ffn2_w_up, ffn2_w_down, ffn2_post_g):
    b, t, d = x.shape
    depth = w_in.shape[0]
    n = b * t
    da_w = DA_HEADS * HEAD_W
    att_w = 3 * da_w
    ret_w = 4 * RET_HEADS * HEAD_W
    sin, cos, log_g = _retention_tables(t)
    bf = lambda w: w.astype(BF16)

    x2d = x.reshape(n, d)
    for l in range(depth):
        lambda_init = 0.8 - 0.6 * math.exp(-0.3 * l)
        w = bf(w_in[l])
        x2d, qt, k, vt, ga, gr, yr = _mix_proj(
            x2d, ffn1_pre_g[l][None], bf(ffn1_w_gate[l]), bf(ffn1_w_up[l]), bf(ffn1_w_down[l]),
            ffn1_post_g[l][None], mix_pre_g[l][None], w[:, att_w:att_w + ret_w], w[:, :att_w],
            w[:, att_w + ret_w:att_w + ret_w + d], w[:, att_w + ret_w + d:],
            sin, cos, log_g, seq_len=t)
        ya = _diff_attn(qt.reshape((b, -1) + qt.shape[1:]), k.reshape(b, t, da_w),
                        vt.reshape((b, -1) + vt.shape[1:]), lambda_q1[l][None], lambda_k1[l][None],
                        lambda_q2[l][None], lambda_k2[l][None], diff_subln_g[l][None], lambda_init)
        x2d = _merge_ffn(x2d, ya.reshape(n, da_w), yr, ga, gr,
                         bf(w_attn_proj[l]), bf(w_ret_proj[l]), bf(w_out[l]), mix_post_g[l][None],
                         ffn2_pre_g[l][None], bf(ffn2_w_gate[l]), bf(ffn2_w_up[l]),
                         bf(ffn2_w_down[l]), ffn2_post_g[l][None])
    return x2d.reshape(b, t, d)
```

```python
import functools
import math

import jax
import jax.numpy as jnp
from jax import lax
from jax.experimental import pallas as pl
from jax.experimental.pallas import tpu as pltpu

F32 = jnp.float32
BF16 = jnp.bfloat16

DA_HEADS = 4
DA_QK = 64
HEAD_W = 128
RET_HEADS = 4
NORM_EPS = 1e-6
SUBLN_EPS = 1e-5
NEG_INF = -1e30
Q_SCALE = DA_QK ** -0.5 * math.log2(math.e)
ONES_ROWS = 16

VMEM_LIMIT_BYTES = 56 * 1024 * 1024

FFN_TOKENS = 512
ROW_GROUPS = 2
ATTN_BLOCK = 256
KEY_BLOCK_GROUP = 4
SCORE_LOOKAHEAD = 2


def _params(*sem):
    return pltpu.CompilerParams(dimension_semantics=sem, vmem_limit_bytes=VMEM_LIMIT_BYTES)


def _resident(shape):
    nd = len(shape)
    return pl.BlockSpec(shape, lambda *_: (0,) * nd, pipeline_mode=pl.Buffered(1))


def _sigmoid(x):
    return 0.5 * jnp.tanh(0.5 * x) + 0.5


def _rmsnorm(x, g, eps):
    return x * lax.rsqrt(jnp.mean(x * x, axis=-1, keepdims=True) + eps) * g


def _ffn_rows(x, pre_g_ref, wg_ref, wu_ref, wd_ref, post_g_ref):
    h = _rmsnorm(x, pre_g_ref[...], NORM_EPS).astype(BF16)
    g = jnp.dot(h, wg_ref[...], preferred_element_type=F32)
    u = jnp.dot(h, wu_ref[...], preferred_element_type=F32)
    a = (g * _sigmoid(g) * u).astype(BF16)
    y = jnp.dot(a, wd_ref[...], preferred_element_type=F32)
    return x + 0.5 * _rmsnorm(y, post_g_ref[...], NORM_EPS)


def _rotate_every_two(x):
    lane = lax.broadcasted_iota(jnp.int32, x.shape, 1)
    nxt = pltpu.roll(x, HEAD_W - 1, 1)
    prv = pltpu.roll(x, 1, 1)
    return jnp.where(lane % 2 == 0, -nxt, prv)


def _retention_stage1(zr, sin, cos, states, xi_ref, zeta_ref, logg_ref, chunk):
    width = RET_HEADS * HEAD_W
    out = []
    for h in range(RET_HEADS):
        q = zr[:, h * HEAD_W:(h + 1) * HEAD_W]
        k = zr[:, width + h * HEAD_W:width + (h + 1) * HEAD_W]
        v = zr[:, 2 * width + h * HEAD_W:2 * width + (h + 1) * HEAD_W].astype(BF16)
        q = q * cos + _rotate_every_two(q) * sin
        k = k * cos + _rotate_every_two(k) * sin
        s = lax.dot_general(q.astype(BF16), k.astype(BF16), (((1,), (1,)), ((), ())),
                            preferred_element_type=F32)
        inter = jnp.dot((q * xi_ref[h]).astype(BF16), states[h].astype(BF16),
                        preferred_element_type=F32)
        kv = lax.dot_general((k * zeta_ref[h]).astype(BF16), v, (((0,), (0,)), ((), ())),
                             preferred_element_type=F32)
        states[h] = jnp.exp(logg_ref[h] * float(chunk)) * states[h] + kv
        out.append((s, inter, v))
    return out


def _retention_stage2(stage1, gate, dmask_ref):
    ys = []
    for h, (s, inter, v) in enumerate(stage1):
        intra = jnp.dot((s * dmask_ref[h]).astype(BF16), v, preferred_element_type=F32)
        y = intra + inter
        y = y * lax.rsqrt(jnp.mean(y * y, axis=-1, keepdims=True) + NORM_EPS)
        g = gate[:, h * HEAD_W:(h + 1) * HEAD_W]
        ys.append(y * (g * _sigmoid(g)))
    return jnp.concatenate(ys, axis=1)


def _mix_proj_kernel(x_ref, pre_g_ref, wg_ref, wu_ref, wd_ref, post_g_ref,
                     g_ref, wr_ref, wa_ref, wga_ref, wgr_ref, sin_ref, cos_ref, logg_ref,
                     x1_ref, qt_ref, k_ref, vt_ref, ga_ref, gr_ref, yr_ref,
                     state_ref, dmask_ref, xi_ref, zeta_ref, *, tiles_per_seq):
    t = pl.program_id(0)
    chunk = x_ref.shape[0] // ROW_GROUPS
    width = RET_HEADS * HEAD_W
    k_scale = HEAD_W ** -0.5

    @pl.when(t == 0)
    def _():
        state_ref[...] = jnp.zeros(state_ref.shape, F32)
        idx = lax.broadcasted_iota(jnp.int32, (chunk, HEAD_W), 0).astype(F32)
        qi = lax.broadcasted_iota(jnp.int32, (chunk, chunk), 0)
        ki = lax.broadcasted_iota(jnp.int32, (chunk, chunk), 1)
        diff = (qi - ki).astype(F32)
        for h in range(RET_HEADS):
            log_g = logg_ref[h]
            xi_ref[h] = jnp.exp(log_g * (idx + 1.0))
            zeta_ref[h] = jnp.exp(log_g * (chunk - 1.0 - idx)) * k_scale
            dmask_ref[h] = jnp.where(
                diff >= 0, jnp.exp(log_g[:, :1] * jnp.maximum(diff, 0.0)), 0.0) * k_scale

    keep = jnp.where(t % tiles_per_seq == 0, 0.0, 1.0)
    states = [state_ref[h] * keep for h in range(RET_HEADS)]

    groups = [slice(r * chunk, (r + 1) * chunk) for r in range(ROW_GROUPS)]
    hs = []
    for sl in groups:
        x1 = _ffn_rows(x_ref[sl, :], pre_g_ref, wg_ref, wu_ref, wd_ref, post_g_ref)
        x1_ref[sl, :] = x1
        hs.append(_rmsnorm(x1, g_ref[...], NORM_EPS).astype(BF16))
    zrs = [jnp.dot(h, wr_ref[...], preferred_element_type=F32) for h in hs]

    def gate_a(r):
        ga = jnp.dot(hs[r], wga_ref[...], preferred_element_type=F32)
        ga_ref[groups[r], :] = _sigmoid(ga).astype(ga_ref.dtype)

    def gate_r(r):
        gr = jnp.dot(hs[r], wgr_ref[...], preferred_element_type=F32)
        gr_ref[groups[r], :] = _sigmoid(gr).astype(gr_ref.dtype)

    def attn_qkv(r):
        da_w = DA_HEADS * HEAD_W
        z = jnp.dot(hs[r], wa_ref[...], preferred_element_type=F32)
        k_ref[groups[r], :] = z[:, da_w:2 * da_w].astype(BF16)
        feat = lax.broadcasted_iota(jnp.int32, (HEAD_W, chunk), 0)
        for h in range(DA_HEADS):
            qt = (z[:, h * HEAD_W:(h + 1) * HEAD_W] * Q_SCALE).T
            qt_ref[r, h] = jnp.concatenate([jnp.where(feat < DA_QK, qt, 0.0),
                                            jnp.where(feat >= DA_QK, qt, 0.0)], axis=1).astype(BF16)
            vt = z[:, 2 * da_w + h * HEAD_W:2 * da_w + (h + 1) * HEAD_W].T
            vt_ref[r, h, :HEAD_W, :] = vt.astype(BF16)
            vt_ref[r, h, HEAD_W:, :] = jnp.ones((ONES_ROWS, chunk), BF16)

    for r in range(ROW_GROUPS):
        gate_a(r)
        stage1 = _retention_stage1(zrs[r], sin_ref[groups[r], :], cos_ref[groups[r], :], states,
                                   xi_ref, zeta_ref, logg_ref, chunk)
        gate_r(r)
        yr = _retention_stage2(stage1, zrs[r][:, 3 * width:], dmask_ref)
        yr_ref[groups[r], :] = yr.astype(yr_ref.dtype)
        attn_qkv(r)
    for h in range(RET_HEADS):
        state_ref[h] = states[h]


def _mix_proj(x2d, pre_g, wg, wu, wd, post_g, g, wr, wa, wga, wgr, sin, cos, log_g, seq_len):
    n, d = x2d.shape
    tm = min(FFN_TOKENS, seq_len)
    chunk = tm // ROW_GROUPS
    tiles_per_seq = seq_len // tm
    ws = (pre_g, wg, wu, wd, post_g, g, wr, wa, wga, wgr)

    def row(w):
        return pl.BlockSpec((tm, w), lambda i: (i, 0))

    table = pl.BlockSpec((tm, HEAD_W), lambda i: (i % tiles_per_seq, 0))
    assert chunk == min(ATTN_BLOCK, seq_len)
    rows = (DA_HEADS * HEAD_W, wga.shape[1], wgr.shape[1], RET_HEADS * HEAD_W)
    qt_shape = (n // chunk, DA_HEADS, HEAD_W, 2 * chunk)
    vt_shape = (n // chunk, DA_HEADS, HEAD_W + ONES_ROWS, chunk)

    def blocks(shape):
        return pl.BlockSpec((ROW_GROUPS,) + shape[1:], lambda i: (i, 0, 0, 0))

    return pl.pallas_call(
        functools.partial(_mix_proj_kernel, tiles_per_seq=tiles_per_seq),
        grid=(n // tm,),
        in_specs=[row(d)] + [_resident(w.shape) for w in ws]
        + [table, table, _resident(log_g.shape)],
        out_specs=[row(d), blocks(qt_shape), row(rows[0]), blocks(vt_shape)] + [row(w) for w in rows[1:]],
        out_shape=[jax.ShapeDtypeStruct((n, d), F32),
                   jax.ShapeDtypeStruct(qt_shape, BF16), jax.ShapeDtypeStruct((n, rows[0]), BF16),
                   jax.ShapeDtypeStruct(vt_shape, BF16)]
        + [jax.ShapeDtypeStruct((n, w), BF16) for w in rows[1:]],
        scratch_shapes=[pltpu.VMEM((RET_HEADS, HEAD_W, HEAD_W), F32),
                        pltpu.VMEM((RET_HEADS, chunk, chunk), F32),
                        pltpu.VMEM((RET_HEADS, chunk, HEAD_W), F32),
                        pltpu.VMEM((RET_HEADS, chunk, HEAD_W), F32)],
        compiler_params=_params("arbitrary"),
        name="mix_proj",
    )(x2d, *ws, sin, cos, log_g)


def _diff_attn_kernel(qt_ref, k_ref, vt_ref, lq1_ref, lk1_ref, lq2_ref, lk2_ref, g_ref,
                      o_ref, m_ref, acc_ref, s_ref, *, blk, lambda_init):
    i = pl.program_id(1)
    heads = range(DA_HEADS)

    m_ref[...] = jnp.full(m_ref.shape, NEG_INF, F32)
    acc_ref[...] = jnp.zeros(acc_ref.shape, F32)

    def scores(j, h):
        start = pl.multiple_of(j * blk, blk)
        k = k_ref[pl.ds(start, blk), h * HEAD_W:(h + 1) * HEAD_W]
        return jnp.dot(k, qt_ref[h], preferred_element_type=F32)

    def step(j, masked):
        ahead = {}
        for h in heads:
            t = h + SCORE_LOOKAHEAD
            if t < DA_HEADS:
                ahead[t] = scores(j, t)
            elif not masked:
                s_ref[t - DA_HEADS] = scores(j + 1, t - DA_HEADS)
            s = s_ref[h] if h < SCORE_LOOKAHEAD else ahead.pop(h)
            if masked:
                key = lax.broadcasted_iota(jnp.int32, s.shape, 0)
                qry = lax.broadcasted_iota(jnp.int32, s.shape, 1)
                qry = jnp.where(qry >= blk, qry - blk, qry)
                s = jnp.where(key <= qry, s, NEG_INF)
            m_old = m_ref[h]
            m_new = jnp.maximum(m_old, jnp.max(s, axis=0, keepdims=True))
            alpha = jnp.exp2(m_old - m_new)
            p = jnp.exp2(s - m_new).astype(BF16)
            pv = jnp.dot(vt_ref[j, h], p, preferred_element_type=F32)
            acc_ref[h] = alpha * acc_ref[h] + pv
            m_ref[h] = m_new

    def group_body(g, carry):
        for u in range(KEY_BLOCK_GROUP):
            step(g * KEY_BLOCK_GROUP + u, False)
        return carry

    def single_body(j, carry):
        step(j, False)
        return carry

    for h in range(SCORE_LOOKAHEAD):
        s_ref[h] = scores(0, h)
    n_groups = i // KEY_BLOCK_GROUP
    lax.fori_loop(0, n_groups, group_body, 0)
    lax.fori_loop(n_groups * KEY_BLOCK_GROUP, i, single_body, 0)
    step(i, True)

    lam = (jnp.exp(jnp.sum(lq1_ref[...] * lk1_ref[...], axis=-1, keepdims=True))
           - jnp.exp(jnp.sum(lq2_ref[...] * lk2_ref[...], axis=-1, keepdims=True))
           + lambda_init)
    for h in heads:
        acc = acc_ref[h]
        ot = acc[:HEAD_W] / acc[HEAD_W:HEAD_W + 1]
        ot = ot[:, :blk] - lam * ot[:, blk:]
        ot = ot * lax.rsqrt(jnp.mean(ot * ot, axis=0, keepdims=True) + SUBLN_EPS)
        o_ref[:, h * HEAD_W:(h + 1) * HEAD_W] = (
            ot.T * g_ref[...] * (1.0 - lambda_init)).astype(o_ref.dtype)


def _diff_attn(qt, k, vt, lq1, lk1, lq2, lk2, subln_g, lambda_init):
    b, t, width = k.shape
    blk = vt.shape[-1]
    small = _resident((1, DA_QK))
    return pl.pallas_call(
        functools.partial(_diff_attn_kernel, blk=blk, lambda_init=lambda_init),
        grid=(b, t // blk),
        in_specs=[
            pl.BlockSpec((None, None) + qt.shape[2:], lambda bi, i: (bi, i, 0, 0, 0)),
            pl.BlockSpec((None, t, width), lambda bi, i: (bi, 0, 0)),
            pl.BlockSpec((None,) + vt.shape[1:], lambda bi, i: (bi, 0, 0, 0, 0)),
            small, small, small, small, _resident((1, HEAD_W)),
        ],
        out_specs=pl.BlockSpec((None, blk, width), lambda bi, i: (bi, i, 0)),
        out_shape=jax.ShapeDtypeStruct((b, t, width), BF16),
        scratch_shapes=[
            pltpu.VMEM((DA_HEADS, 1, 2 * blk), F32),
            pltpu.VMEM((DA_HEADS, HEAD_W + ONES_ROWS, 2 * blk), F32),
            pltpu.VMEM((SCORE_LOOKAHEAD, blk, 2 * blk), F32),
        ],
        compiler_params=_params("parallel", "arbitrary"),
        name="diff_attn",
    )(qt, k, vt, lq1, lk1, lq2, lk2, subln_g)


def _merge_ffn_kernel(x_ref, ya_ref, yr_ref, ga_ref, gr_ref, wa_ref, wr_ref, wo_ref, mix_g_ref,
                      pre_g_ref, wg_ref, wu_ref, wd_ref, post_g_ref, o_ref):
    rows = x_ref.shape[0] // ROW_GROUPS
    groups = [slice(r * rows, (r + 1) * rows) for r in range(ROW_GROUPS)]
    xs = []
    for sl in groups:
        ya = jnp.dot(ya_ref[sl, :], wa_ref[...], preferred_element_type=F32)
        yr = jnp.dot(yr_ref[sl, :], wr_ref[...], preferred_element_type=F32)
        merged = ga_ref[sl, :].astype(F32) * ya + gr_ref[sl, :].astype(F32) * yr
        y = jnp.dot(merged.astype(BF16), wo_ref[...], preferred_element_type=F32)
        xs.append(x_ref[sl, :] + _rmsnorm(y, mix_g_ref[...], NORM_EPS))
    for sl, x in zip(groups, xs):
        o_ref[sl, :] = _ffn_rows(x, pre_g_ref, wg_ref, wu_ref, wd_ref, post_g_ref)


def _merge_ffn(x2d, ya, yr, ga, gr, wa, wr, wo, mix_g, pre_g, wg, wu, wd, post_g):
    n, d = x2d.shape
    tm = min(FFN_TOKENS, n)

    def row(w):
        return pl.BlockSpec((tm, w), lambda i: (i, 0))

    weights = (wa, wr, wo, mix_g, pre_g, wg, wu, wd, post_g)
    return pl.pallas_call(
        _merge_ffn_kernel,
        grid=(n // tm,),
        in_specs=[row(d), row(ya.shape[1]), row(yr.shape[1]), row(d), row(d)]
        + [_resident(w.shape) for w in weights],
        out_specs=row(d),
        out_shape=jax.ShapeDtypeStruct((n, d), F32),
        compiler_params=_params("parallel"),
        name="merge_ffn",
    )(x2d, ya, yr, ga, gr, *weights)


def _retention_tables(t):
    angle = 1.0 / (10000.0 ** jnp.linspace(0.0, 1.0, HEAD_W // 2, dtype=F32))
    angle = jnp.repeat(angle, 2)
    ang = jnp.arange(t, dtype=F32)[:, None] * angle[None, :]
    log_g = jnp.log(1.0 - 2.0 ** (-5.0 - jnp.arange(RET_HEADS, dtype=F32)))
    log_g = jnp.broadcast_to(log_g[:, None, None], (RET_HEADS, 1, HEAD_W))
    return jnp.sin(ang), jnp.cos(ang), log_g


def kernel(x, ffn1_pre_g, ffn1_w_gate, ffn1_w_up, ffn1_w_down, ffn1_post_g, mix_pre_g, w_in, lambda_q1, lambda_k1, lambda_q2, lambda_k2, diff_subln_g, w_attn_proj, w_ret_proj, w_out, mix_post_g, ffn2_pre_g, ffn2_w_gate, ffn2_w_up, ffn2_w_down, ffn2_post_g):
    b, t, d = x.shape
    depth = w_in.shape[0]
    n = b * t
    da_w = DA_HEADS * HEAD_W
    att_w = 3 * da_w
    ret_w = 4 * RET_HEADS * HEAD_W
    sin, cos, log_g = _retention_tables(t)
    bf = lambda w: w.astype(BF16)

    x2d = x.reshape(n, d)
    for l in range(depth):
        lambda_init = 0.8 - 0.6 * math.exp(-0.3 * l)
        w = bf(w_in[l])
        x2d, qt, k, vt, ga, gr, yr = _mix_proj(
            x2d, ffn1_pre_g[l][None], bf(ffn1_w_gate[l]), bf(ffn1_w_up[l]), bf(ffn1_w_down[l]),
            ffn1_post_g[l][None], mix_pre_g[l][None], w[:, att_w:att_w + ret_w], w[:, :att_w],
            w[:, att_w + ret_w:att_w + ret_w + d], w[:, att_w + ret_w + d:],
            sin, cos, log_g, seq_len=t)
        ya = _diff_attn(qt.reshape((b, -1) + qt.shape[1:]), k.reshape(b, t, da_w),
                        vt.reshape((b, -1) + vt.shape[1:]), lambda_q1[l][None], lambda_k1[l][None],
                        lambda_q2[l][None], lambda_k2[l][None], diff_subln_g[l][None], lambda_init)
        x2d = _merge_ffn(x2d, ya.reshape(n, da_w), yr, ga, gr,
                         bf(w_attn_proj[l]), bf(w_ret_proj[l]), bf(w_out[l]), mix_post_g[l][None],
                         ffn2_pre_g[l][None], bf(ffn2_w_gate[l]), bf(ffn2_w_up[l]),
                         bf(ffn2_w_down[l]), ffn2_post_g[l][None])
    return x2d.reshape(b, t, d)
```

```python
import functools
import math

import jax
import jax.numpy as jnp
from jax import lax
from jax.experimental import pallas as pl
from jax.experimental.pallas import tpu as pltpu

F32 = jnp.float32
BF16 = jnp.bfloat16

DA_HEADS = 4
DA_QK = 64
HEAD_W = 128
RET_HEADS = 4
NORM_EPS = 1e-6
SUBLN_EPS = 1e-5
NEG_INF = -1e30
Q_SCALE = DA_QK ** -0.5 * math.log2(math.e)
ONES_ROWS = 16

VMEM_LIMIT_BYTES = 56 * 1024 * 1024

FFN_TOKENS = 512
ROW_GROUPS = 2
ATTN_BLOCK = 256
ATTN_BATCHES = 2
KEY_BLOCK_GROUP = 4
SCORE_LOOKAHEAD = 2


def _params(*sem):
    return pltpu.CompilerParams(dimension_semantics=sem, vmem_limit_bytes=VMEM_LIMIT_BYTES)


def _resident(shape):
    nd = len(shape)
    return pl.BlockSpec(shape, lambda *_: (0,) * nd, pipeline_mode=pl.Buffered(1))


def _sigmoid(x):
    return 0.5 * jnp.tanh(0.5 * x) + 0.5


def _rmsnorm(x, g, eps):
    return x * lax.rsqrt(jnp.mean(x * x, axis=-1, keepdims=True) + eps) * g


def _ffn_rows(x, pre_g_ref, wg_ref, wu_ref, wd_ref, post_g_ref):
    h = _rmsnorm(x, pre_g_ref[...], NORM_EPS).astype(BF16)
    g = jnp.dot(h, wg_ref[...], preferred_element_type=F32)
    u = jnp.dot(h, wu_ref[...], preferred_element_type=F32)
    a = (g * _sigmoid(g) * u).astype(BF16)
    y = jnp.dot(a, wd_ref[...], preferred_element_type=F32)
    return x + 0.5 * _rmsnorm(y, post_g_ref[...], NORM_EPS)


def _rotate_every_two(x):
    lane = lax.broadcasted_iota(jnp.int32, x.shape, 1)
    nxt = pltpu.roll(x, HEAD_W - 1, 1)
    prv = pltpu.roll(x, 1, 1)
    return jnp.where(lane % 2 == 0, -nxt, prv)


def _retention_stage1(zr, sin, cos, states, xi_ref, zeta_ref, logg_ref, chunk):
    width = RET_HEADS * HEAD_W
    out = []
    for h in range(RET_HEADS):
        q = zr[:, h * HEAD_W:(h + 1) * HEAD_W]
        k = zr[:, width + h * HEAD_W:width + (h + 1) * HEAD_W]
        v = zr[:, 2 * width + h * HEAD_W:2 * width + (h + 1) * HEAD_W].astype(BF16)
        q = q * cos + _rotate_every_two(q) * sin
        k = k * cos + _rotate_every_two(k) * sin
        s = lax.dot_general(q.astype(BF16), k.astype(BF16), (((1,), (1,)), ((), ())),
                            preferred_element_type=F32)
        inter = jnp.dot((q * xi_ref[h]).astype(BF16), states[h].astype(BF16),
                        preferred_element_type=F32)
        kv = lax.dot_general((k * zeta_ref[h]).astype(BF16), v, (((0,), (0,)), ((), ())),
                             preferred_element_type=F32)
        states[h] = jnp.exp(logg_ref[h] * float(chunk)) * states[h] + kv
        out.append((s, inter, v))
    return out


def _retention_stage2(stage1, gate, dmask_ref):
    ys = []
    for h, (s, inter, v) in enumerate(stage1):
        intra = jnp.dot((s * dmask_ref[h]).astype(BF16), v, preferred_element_type=F32)
        y = intra + inter
        y = y * lax.rsqrt(jnp.mean(y * y, axis=-1, keepdims=True) + NORM_EPS)
        g = gate[:, h * HEAD_W:(h + 1) * HEAD_W]
        ys.append(y * (g * _sigmoid(g)))
    return jnp.concatenate(ys, axis=1)


def _mix_proj_kernel(x_ref, pre_g_ref, wg_ref, wu_ref, wd_ref, post_g_ref,
                     g_ref, wr_ref, wa_ref, wga_ref, wgr_ref, sin_ref, cos_ref, logg_ref,
                     x1_ref, qt_ref, k_ref, vt_ref, ga_ref, gr_ref, yr_ref,
                     state_ref, dmask_ref, xi_ref, zeta_ref, *, tiles_per_seq):
    t = pl.program_id(0)
    chunk = x_ref.shape[0] // ROW_GROUPS
    width = RET_HEADS * HEAD_W
    k_scale = HEAD_W ** -0.5

    @pl.when(t == 0)
    def _():
        state_ref[...] = jnp.zeros(state_ref.shape, F32)
        idx = lax.broadcasted_iota(jnp.int32, (chunk, HEAD_W), 0).astype(F32)
        qi = lax.broadcasted_iota(jnp.int32, (chunk, chunk), 0)
        ki = lax.broadcasted_iota(jnp.int32, (chunk, chunk), 1)
        diff = (qi - ki).astype(F32)
        for h in range(RET_HEADS):
            log_g = logg_ref[h]
            xi_ref[h] = jnp.exp(log_g * (idx + 1.0))
            zeta_ref[h] = jnp.exp(log_g * (chunk - 1.0 - idx)) * k_scale
            dmask_ref[h] = jnp.where(
                diff >= 0, jnp.exp(log_g[:, :1] * jnp.maximum(diff, 0.0)), 0.0) * k_scale

    keep = jnp.where(t % tiles_per_seq == 0, 0.0, 1.0)
    states = [state_ref[h] * keep for h in range(RET_HEADS)]

    groups = [slice(r * chunk, (r + 1) * chunk) for r in range(ROW_GROUPS)]
    hs = []
    for sl in groups:
        x1 = _ffn_rows(x_ref[sl, :], pre_g_ref, wg_ref, wu_ref, wd_ref, post_g_ref)
        x1_ref[sl, :] = x1
        hs.append(_rmsnorm(x1, g_ref[...], NORM_EPS).astype(BF16))
    zrs = [jnp.dot(h, wr_ref[...], preferred_element_type=F32) for h in hs]

    def gate_a(r):
        ga = jnp.dot(hs[r], wga_ref[...], preferred_element_type=F32)
        ga_ref[groups[r], :] = _sigmoid(ga).astype(ga_ref.dtype)

    def gate_r(r):
        gr = jnp.dot(hs[r], wgr_ref[...], preferred_element_type=F32)
        gr_ref[groups[r], :] = _sigmoid(gr).astype(gr_ref.dtype)

    def attn_qkv(r):
        da_w = DA_HEADS * HEAD_W
        z = jnp.dot(hs[r], wa_ref[...], preferred_element_type=F32)
        k_ref[groups[r], :] = z[:, da_w:2 * da_w].astype(BF16)
        feat = lax.broadcasted_iota(jnp.int32, (HEAD_W, chunk), 0)
        for h in range(DA_HEADS):
            qt = (z[:, h * HEAD_W:(h + 1) * HEAD_W] * Q_SCALE).T
            qt_ref[r, h] = jnp.concatenate([jnp.where(feat < DA_QK, qt, 0.0),
                                            jnp.where(feat >= DA_QK, qt, 0.0)], axis=1).astype(BF16)
            vt = z[:, 2 * da_w + h * HEAD_W:2 * da_w + (h + 1) * HEAD_W].T
            vt_ref[r, h, :HEAD_W, :] = vt.astype(BF16)
            vt_ref[r, h, HEAD_W:, :] = jnp.ones((ONES_ROWS, chunk), BF16)

    for r in range(ROW_GROUPS):
        gate_a(r)
        stage1 = _retention_stage1(zrs[r], sin_ref[groups[r], :], cos_ref[groups[r], :], states,
                                   xi_ref, zeta_ref, logg_ref, chunk)
        gate_r(r)
        yr = _retention_stage2(stage1, zrs[r][:, 3 * width:], dmask_ref)
        yr_ref[groups[r], :] = yr.astype(yr_ref.dtype)
        attn_qkv(r)
    for h in range(RET_HEADS):
        state_ref[h] = states[h]


def _mix_proj(x2d, pre_g, wg, wu, wd, post_g, g, wr, wa, wga, wgr, sin, cos, log_g, seq_len):
    n, d = x2d.shape
    tm = min(FFN_TOKENS, seq_len)
    chunk = tm // ROW_GROUPS
    tiles_per_seq = seq_len // tm
    ws = (pre_g, wg, wu, wd, post_g, g, wr, wa, wga, wgr)

    def row(w):
        return pl.BlockSpec((tm, w), lambda i: (i, 0))

    table = pl.BlockSpec((tm, HEAD_W), lambda i: (i % tiles_per_seq, 0))
    assert chunk == min(ATTN_BLOCK, seq_len)
    rows = (DA_HEADS * HEAD_W, wga.shape[1], wgr.shape[1], RET_HEADS * HEAD_W)
    qt_shape = (n // chunk, DA_HEADS, HEAD_W, 2 * chunk)
    vt_shape = (n // chunk, DA_HEADS, HEAD_W + ONES_ROWS, chunk)

    def blocks(shape):
        return pl.BlockSpec((ROW_GROUPS,) + shape[1:], lambda i: (i, 0, 0, 0))

    return pl.pallas_call(
        functools.partial(_mix_proj_kernel, tiles_per_seq=tiles_per_seq),
        grid=(n // tm,),
        in_specs=[row(d)] + [_resident(w.shape) for w in ws]
        + [table, table, _resident(log_g.shape)],
        out_specs=[row(d), blocks(qt_shape), row(rows[0]), blocks(vt_shape)] + [row(w) for w in rows[1:]],
        out_shape=[jax.ShapeDtypeStruct((n, d), F32),
                   jax.ShapeDtypeStruct(qt_shape, BF16), jax.ShapeDtypeStruct((n, rows[0]), BF16),
                   jax.ShapeDtypeStruct(vt_shape, BF16)]
        + [jax.ShapeDtypeStruct((n, w), BF16) for w in rows[1:]],
        scratch_shapes=[pltpu.VMEM((RET_HEADS, HEAD_W, HEAD_W), F32),
                        pltpu.VMEM((RET_HEADS, chunk, chunk), F32),
                        pltpu.VMEM((RET_HEADS, chunk, HEAD_W), F32),
                        pltpu.VMEM((RET_HEADS, chunk, HEAD_W), F32)],
        compiler_params=_params("arbitrary"),
        name="mix_proj",
    )(x2d, *ws, sin, cos, log_g)


def _diff_attn_kernel(qt_ref, k_ref, vt_ref, lq1_ref, lk1_ref, lq2_ref, lk2_ref, g_ref,
                      o_ref, m_ref, acc_ref, s_ref, *, blk, lambda_init):
    i = pl.program_id(1)
    chains = [(b, h) for b in range(qt_ref.shape[0]) for h in range(DA_HEADS)]
    n_chains = len(chains)

    m_ref[...] = jnp.full(m_ref.shape, NEG_INF, F32)
    acc_ref[...] = jnp.zeros(acc_ref.shape, F32)

    def scores(j, c):
        b, h = chains[c]
        start = pl.multiple_of(j * blk, blk)
        k = k_ref[b, pl.ds(start, blk), h * HEAD_W:(h + 1) * HEAD_W]
        return jnp.dot(k, qt_ref[b, h], preferred_element_type=F32)

    def step(j, masked):
        ahead = {}
        for c, (b, h) in enumerate(chains):
            t = c + SCORE_LOOKAHEAD
            if t < n_chains:
                ahead[t] = scores(j, t)
            elif not masked:
                s_ref[t - n_chains] = scores(j + 1, t - n_chains)
            s = s_ref[c] if c < SCORE_LOOKAHEAD else ahead.pop(c)
            if masked:
                key = lax.broadcasted_iota(jnp.int32, s.shape, 0)
                qry = lax.broadcasted_iota(jnp.int32, s.shape, 1)
                qry = jnp.where(qry >= blk, qry - blk, qry)
                s = jnp.where(key <= qry, s, NEG_INF)
            m_old = m_ref[c]
            m_new = jnp.maximum(m_old, jnp.max(s, axis=0, keepdims=True))
            alpha = jnp.exp2(m_old - m_new)
            p = jnp.exp2(s - m_new).astype(BF16)
            pv = jnp.dot(vt_ref[b, j, h], p, preferred_element_type=F32)
            acc_ref[c] = alpha * acc_ref[c] + pv
            m_ref[c] = m_new

    def group_body(g, carry):
        for u in range(KEY_BLOCK_GROUP):
            step(g * KEY_BLOCK_GROUP + u, False)
        return carry

    def single_body(j, carry):
        step(j, False)
        return carry

    for c in range(SCORE_LOOKAHEAD):
        s_ref[c] = scores(0, c)
    n_groups = i // KEY_BLOCK_GROUP
    lax.fori_loop(0, n_groups, group_body, 0)
    lax.fori_loop(n_groups * KEY_BLOCK_GROUP, i, single_body, 0)
    step(i, True)

    lam = (jnp.exp(jnp.sum(lq1_ref[...] * lk1_ref[...], axis=-1, keepdims=True))
           - jnp.exp(jnp.sum(lq2_ref[...] * lk2_ref[...], axis=-1, keepdims=True))
           + lambda_init)
    for c, (b, h) in enumerate(chains):
        acc = acc_ref[c]
        ot = acc[:HEAD_W] / acc[HEAD_W:HEAD_W + 1]
        ot = ot[:, :blk] - lam * ot[:, blk:]
        ot = ot * lax.rsqrt(jnp.mean(ot * ot, axis=0, keepdims=True) + SUBLN_EPS)
        o_ref[b, :, h * HEAD_W:(h + 1) * HEAD_W] = (
            ot.T * g_ref[...] * (1.0 - lambda_init)).astype(o_ref.dtype)


def _diff_attn(qt, k, vt, lq1, lk1, lq2, lk2, subln_g, lambda_init):
    b, t, width = k.shape
    blk = vt.shape[-1]
    nb = ATTN_BATCHES if b % ATTN_BATCHES == 0 else 1
    small = _resident((1, DA_QK))
    return pl.pallas_call(
        functools.partial(_diff_attn_kernel, blk=blk, lambda_init=lambda_init),
        grid=(b // nb, t // blk),
        in_specs=[
            pl.BlockSpec((nb, None) + qt.shape[2:], lambda bi, i: (bi, i, 0, 0, 0)),
            pl.BlockSpec((nb, t, width), lambda bi, i: (bi, 0, 0)),
            pl.BlockSpec((nb,) + vt.shape[1:], lambda bi, i: (bi, 0, 0, 0, 0)),
            small, small, small, small, _resident((1, HEAD_W)),
        ],
        out_specs=pl.BlockSpec((nb, blk, width), lambda bi, i: (bi, i, 0)),
        out_shape=jax.ShapeDtypeStruct((b, t, width), BF16),
        scratch_shapes=[
            pltpu.VMEM((nb * DA_HEADS, 1, 2 * blk), F32),
            pltpu.VMEM((nb * DA_HEADS, HEAD_W + ONES_ROWS, 2 * blk), F32),
            pltpu.VMEM((SCORE_LOOKAHEAD, blk, 2 * blk), F32),
        ],
        compiler_params=_params("parallel", "arbitrary"),
        name="diff_attn",
    )(qt, k, vt, lq1, lk1, lq2, lk2, subln_g)


def _merge_ffn_kernel(x_ref, ya_ref, yr_ref, ga_ref, gr_ref, wa_ref, wr_ref, wo_ref, mix_g_ref,
                      pre_g_ref, wg_ref, wu_ref, wd_ref, post_g_ref, o_ref):
    rows = x_ref.shape[0] // ROW_GROUPS
    groups = [slice(r * rows, (r + 1) * rows) for r in range(ROW_GROUPS)]
    xs = []
    for sl in groups:
        ya = jnp.dot(ya_ref[sl, :], wa_ref[...], preferred_element_type=F32)
        yr = jnp.dot(yr_ref[sl, :], wr_ref[...], preferred_element_type=F32)
        merged = ga_ref[sl, :].astype(F32) * ya + gr_ref[sl, :].astype(F32) * yr
        y = jnp.dot(merged.astype(BF16), wo_ref[...], preferred_element_type=F32)
        xs.append(x_ref[sl, :] + _rmsnorm(y, mix_g_ref[...], NORM_EPS))
    for sl, x in zip(groups, xs):
        o_ref[sl, :] = _ffn_rows(x, pre_g_ref, wg_ref, wu_ref, wd_ref, post_g_ref)


def _merge_ffn(x2d, ya, yr, ga, gr, wa, wr, wo, mix_g, pre_g, wg, wu, wd, post_g):
    n, d = x2d.shape
    tm = min(FFN_TOKENS, n)

    def row(w):
        return pl.BlockSpec((tm, w), lambda i: (i, 0))

    weights = (wa, wr, wo, mix_g, pre_g, wg, wu, wd, post_g)
    return pl.pallas_call(
        _merge_ffn_kernel,
        grid=(n // tm,),
        in_specs=[row(d), row(ya.shape[1]), row(yr.shape[1]), row(d), row(d)]
        + [_resident(w.shape) for w in weights],
        out_specs=row(d),
        out_shape=jax.ShapeDtypeStruct((n, d), F32),
        compiler_params=_params("parallel"),
        name="merge_ffn",
    )(x2d, ya, yr, ga, gr, *weights)


def _retention_tables(t):
    angle = 1.0 / (10000.0 ** jnp.linspace(0.0, 1.0, HEAD_W // 2, dtype=F32))
    angle = jnp.repeat(angle, 2)
    ang = jnp.arange(t, dtype=F32)[:, None] * angle[None, :]
    log_g = jnp.log(1.0 - 2.0 ** (-5.0 - jnp.arange(RET_HEADS, dtype=F32)))
    log_g = jnp.broadcast_to(log_g[:, None, None], (RET_HEADS, 1, HEAD_W))
    return jnp.sin(ang), jnp.cos(ang), log_g


def kernel(x, ffn1_pre_g, ffn1_w_gate, ffn1_w_up, ffn1_w_down, ffn1_post_g, mix_pre_g, w_in, lambda_q1, lambda_k1, lambda_q2, lambda_k2, diff_subln_g, w_attn_proj, w_ret_proj, w_out, mix_post_g, ffn2_pre_g, ffn2_w_gate, ffn2_w_up, ffn2_w_down, ffn2_post_g):
    b, t, d = x.shape
    depth = w_in.shape[0]
    n = b * t
    da_w = DA_HEADS * HEAD_W
    att_w = 3 * da_w
    ret_w = 4 * RET_HEADS * HEAD_W
    sin, cos, log_g = _retention_tables(t)
    bf = lambda w: w.astype(BF16)

    x2d = x.reshape(n, d)
    for l in range(depth):
        lambda_init = 0.8 - 0.6 * math.exp(-0.3 * l)
        w = bf(w_in[l])
        x2d, qt, k, vt, ga, gr, yr = _mix_proj(
            x2d, ffn1_pre_g[l][None], bf(ffn1_w_gate[l]), bf(ffn1_w_up[l]), bf(ffn1_w_down[l]),
            ffn1_post_g[l][None], mix_pre_g[l][None], w[:, att_w:att_w + ret_w], w[:, :att_w],
            w[:, att_w + ret_w:att_w + ret_w + d], w[:, att_w + ret_w + d:],
            sin, cos, log_g, seq_len=t)
        ya = _diff_attn(qt.reshape((b, -1) + qt.shape[1:]), k.reshape(b, t, da_w),
                        vt.reshape((b, -1) + vt.shape[1:]), lambda_q1[l][None], lambda_k1[l][None],
                        lambda_q2[l][None], lambda_k2[l][None], diff_subln_g[l][None], lambda_init)
        x2d = _merge_ffn(x2d, ya.reshape(n, da_w), yr, ga, gr,
                         bf(w_attn_proj[l]), bf(w_ret_proj[l]), bf(w_out[l]), mix_post_g[l][None],
                         ffn2_pre_g[l][None], bf(ffn2_w_gate[l]), bf(ffn2_w_up[l]),
                         bf(ffn2_w_down[l]), ffn2_post_g[l][None])
    return x2d.reshape(b, t, d)
```

```python
import functools
import math

import jax
import jax.numpy as jnp
from jax import lax
from jax.experimental import pallas as pl
from jax.experimental.pallas import tpu as pltpu

F32 = jnp.float32
BF16 = jnp.bfloat16

DA_HEADS = 4
DA_QK = 64
HEAD_W = 128
RET_HEADS = 4
NORM_EPS = 1e-6
SUBLN_EPS = 1e-5
NEG_INF = -1e30
Q_SCALE = DA_QK ** -0.5 * math.log2(math.e)
ONES_ROWS = 16

VMEM_LIMIT_BYTES = 56 * 1024 * 1024

FFN_TOKENS = 512
ROW_GROUPS = 2
ATTN_BLOCK = 256
ATTN_BATCHES = 2
KEY_BLOCK_GROUP = 4
SCORE_LOOKAHEAD = 2


def _params(*sem):
    return pltpu.CompilerParams(dimension_semantics=sem, vmem_limit_bytes=VMEM_LIMIT_BYTES)


def _resident(shape):
    nd = len(shape)
    return pl.BlockSpec(shape, lambda *_: (0,) * nd, pipeline_mode=pl.Buffered(1))


def _sigmoid(x):
    return 0.5 * jnp.tanh(0.5 * x) + 0.5


def _rmsnorm(x, g, eps):
    return x * lax.rsqrt(jnp.mean(x * x, axis=-1, keepdims=True) + eps) * g


def _ffn_rows(x, pre_g_ref, wg_ref, wu_ref, wd_ref, post_g_ref):
    h = _rmsnorm(x, pre_g_ref[...], NORM_EPS).astype(BF16)
    g = jnp.dot(h, wg_ref[...], preferred_element_type=F32)
    u = jnp.dot(h, wu_ref[...], preferred_element_type=F32)
    a = (g * _sigmoid(g) * u).astype(BF16)
    y = jnp.dot(a, wd_ref[...], preferred_element_type=F32)
    return x + 0.5 * _rmsnorm(y, post_g_ref[...], NORM_EPS)


def _rotate_every_two(x):
    lane = lax.broadcasted_iota(jnp.int32, x.shape, 1)
    nxt = pltpu.roll(x, HEAD_W - 1, 1)
    prv = pltpu.roll(x, 1, 1)
    return jnp.where(lane % 2 == 0, -nxt, prv)


def _retention_stage1(zr, sin, cos, states, xi_ref, zeta_ref, logg_ref, chunk):
    width = RET_HEADS * HEAD_W
    out = []
    for h in range(RET_HEADS):
        q = zr[:, h * HEAD_W:(h + 1) * HEAD_W]
        k = zr[:, width + h * HEAD_W:width + (h + 1) * HEAD_W]
        v = zr[:, 2 * width + h * HEAD_W:2 * width + (h + 1) * HEAD_W].astype(BF16)
        q = q * cos + _rotate_every_two(q) * sin
        k = k * cos + _rotate_every_two(k) * sin
        s = lax.dot_general(q.astype(BF16), k.astype(BF16), (((1,), (1,)), ((), ())),
                            preferred_element_type=F32)
        inter = jnp.dot((q * xi_ref[h]).astype(BF16), states[h].astype(BF16),
                        preferred_element_type=F32)
        kv = lax.dot_general((k * zeta_ref[h]).astype(BF16), v, (((0,), (0,)), ((), ())),
                             preferred_element_type=F32)
        states[h] = jnp.exp(logg_ref[h] * float(chunk)) * states[h] + kv
        out.append((s, inter, v))
    return out


def _retention_stage2(stage1, gate, dmask_ref):
    ys = []
    for h, (s, inter, v) in enumerate(stage1):
        intra = jnp.dot((s * dmask_ref[h]).astype(BF16), v, preferred_element_type=F32)
        y = intra + inter
        y = y * lax.rsqrt(jnp.mean(y * y, axis=-1, keepdims=True) + NORM_EPS)
        g = gate[:, h * HEAD_W:(h + 1) * HEAD_W]
        ys.append(y * (g * _sigmoid(g)))
    return jnp.concatenate(ys, axis=1)


def _mix_proj_kernel(x_ref, pre_g_ref, wg_ref, wu_ref, wd_ref, post_g_ref,
                     g_ref, w_ref, sin_ref, cos_ref, logg_ref,
                     x1_ref, qt_ref, k_ref, vt_ref, ga_ref, gr_ref, yr_ref,
                     state_ref, dmask_ref, xi_ref, zeta_ref, *, tiles_per_seq):
    t = pl.program_id(0)
    chunk = x_ref.shape[0] // ROW_GROUPS
    width = RET_HEADS * HEAD_W
    k_scale = HEAD_W ** -0.5
    d_model = x_ref.shape[1]
    att_w = 3 * DA_HEADS * HEAD_W
    wa_ref = w_ref.at[:, :att_w]
    wr_ref = w_ref.at[:, att_w:att_w + 4 * width]
    wga_ref = w_ref.at[:, att_w + 4 * width:att_w + 4 * width + d_model]
    wgr_ref = w_ref.at[:, att_w + 4 * width + d_model:]

    @pl.when(t == 0)
    def _():
        state_ref[...] = jnp.zeros(state_ref.shape, F32)
        idx = lax.broadcasted_iota(jnp.int32, (chunk, HEAD_W), 0).astype(F32)
        qi = lax.broadcasted_iota(jnp.int32, (chunk, chunk), 0)
        ki = lax.broadcasted_iota(jnp.int32, (chunk, chunk), 1)
        diff = (qi - ki).astype(F32)
        for h in range(RET_HEADS):
            log_g = logg_ref[h]
            xi_ref[h] = jnp.exp(log_g * (idx + 1.0))
            zeta_ref[h] = jnp.exp(log_g * (chunk - 1.0 - idx)) * k_scale
            dmask_ref[h] = jnp.where(
                diff >= 0, jnp.exp(log_g[:, :1] * jnp.maximum(diff, 0.0)), 0.0) * k_scale

    keep = jnp.where(t % tiles_per_seq == 0, 0.0, 1.0)
    states = [state_ref[h] * keep for h in range(RET_HEADS)]

    groups = [slice(r * chunk, (r + 1) * chunk) for r in range(ROW_GROUPS)]
    hs = []
    for sl in groups:
        x1 = _ffn_rows(x_ref[sl, :], pre_g_ref, wg_ref, wu_ref, wd_ref, post_g_ref)
        x1_ref[sl, :] = x1
        hs.append(_rmsnorm(x1, g_ref[...], NORM_EPS).astype(BF16))
    zrs = [jnp.dot(h, wr_ref[...], preferred_element_type=F32) for h in hs]

    def gate_a(r):
        ga = jnp.dot(hs[r], wga_ref[...], preferred_element_type=F32)
        ga_ref[groups[r], :] = _sigmoid(ga).astype(ga_ref.dtype)

    def gate_r(r):
        gr = jnp.dot(hs[r], wgr_ref[...], preferred_element_type=F32)
        gr_ref[groups[r], :] = _sigmoid(gr).astype(gr_ref.dtype)

    def attn_qkv(r):
        da_w = DA_HEADS * HEAD_W
        z = jnp.dot(hs[r], wa_ref[...], preferred_element_type=F32)
        k_ref[groups[r], :] = z[:, da_w:2 * da_w].astype(BF16)
        feat = lax.broadcasted_iota(jnp.int32, (HEAD_W, chunk), 0)
        for h in range(DA_HEADS):
            qt = (z[:, h * HEAD_W:(h + 1) * HEAD_W] * Q_SCALE).T
            qt_ref[r, h] = jnp.concatenate([jnp.where(feat < DA_QK, qt, 0.0),
                                            jnp.where(feat >= DA_QK, qt, 0.0)], axis=1).astype(BF16)
            vt = z[:, 2 * da_w + h * HEAD_W:2 * da_w + (h + 1) * HEAD_W].T
            vt_ref[r, h, :HEAD_W, :] = vt.astype(BF16)
            vt_ref[r, h, HEAD_W:, :] = jnp.ones((ONES_ROWS, chunk), BF16)

    for r in range(ROW_GROUPS):
        gate_a(r)
        stage1 = _retention_stage1(zrs[r], sin_ref[groups[r], :], cos_ref[groups[r], :], states,
                                   xi_ref, zeta_ref, logg_ref, chunk)
        attn_qkv(r)
        yr = _retention_stage2(stage1, zrs[r][:, 3 * width:], dmask_ref)
        yr_ref[groups[r], :] = yr.astype(yr_ref.dtype)
        gate_r(r)
    for h in range(RET_HEADS):
        state_ref[h] = states[h]


def _mix_proj(x2d, pre_g, wg, wu, wd, post_g, g, w_in, sin, cos, log_g, seq_len):
    n, d = x2d.shape
    tm = min(FFN_TOKENS, seq_len)
    chunk = tm // ROW_GROUPS
    tiles_per_seq = seq_len // tm
    ws = (pre_g, wg, wu, wd, post_g, g, w_in)

    def row(w):
        return pl.BlockSpec((tm, w), lambda i: (i, 0))

    table = pl.BlockSpec((tm, HEAD_W), lambda i: (i % tiles_per_seq, 0))
    assert chunk == min(ATTN_BLOCK, seq_len)
    rows = (DA_HEADS * HEAD_W, d, d, RET_HEADS * HEAD_W)
    qt_shape = (n // chunk, DA_HEADS, HEAD_W, 2 * chunk)
    vt_shape = (n // chunk, DA_HEADS, HEAD_W + ONES_ROWS, chunk)

    def blocks(shape):
        return pl.BlockSpec((ROW_GROUPS,) + shape[1:], lambda i: (i, 0, 0, 0))

    return pl.pallas_call(
        functools.partial(_mix_proj_kernel, tiles_per_seq=tiles_per_seq),
        grid=(n // tm,),
        in_specs=[row(d)] + [_resident(w.shape) for w in ws]
        + [table, table, _resident(log_g.shape)],
        out_specs=[row(d), blocks(qt_shape), row(rows[0]), blocks(vt_shape)] + [row(w) for w in rows[1:]],
        out_shape=[jax.ShapeDtypeStruct((n, d), F32),
                   jax.ShapeDtypeStruct(qt_shape, BF16), jax.ShapeDtypeStruct((n, rows[0]), BF16),
                   jax.ShapeDtypeStruct(vt_shape, BF16)]
        + [jax.ShapeDtypeStruct((n, w), BF16) for w in rows[1:]],
        scratch_shapes=[pltpu.VMEM((RET_HEADS, HEAD_W, HEAD_W), F32),
                        pltpu.VMEM((RET_HEADS, chunk, chunk), F32),
                        pltpu.VMEM((RET_HEADS, chunk, HEAD_W), F32),
                        pltpu.VMEM((RET_HEADS, chunk, HEAD_W), F32)],
        compiler_params=_params("arbitrary"),
        name="mix_proj",
    )(x2d, *ws, sin, cos, log_g)


def _diff_attn_kernel(qt_ref, k_ref, vt_ref, lq1_ref, lk1_ref, lq2_ref, lk2_ref, g_ref,
                      o_ref, m_ref, acc_ref, s_ref, *, blk, lambda_init):
    i = pl.program_id(1)
    chains = [(b, h) for b in range(qt_ref.shape[0]) for h in range(DA_HEADS)]
    n_chains = len(chains)

    m_ref[...] = jnp.full(m_ref.shape, NEG_INF, F32)
    acc_ref[...] = jnp.zeros(acc_ref.shape, F32)

    def scores(j, c):
        b, h = chains[c]
        start = pl.multiple_of(j * blk, blk)
        k = k_ref[b, pl.ds(start, blk), h * HEAD_W:(h + 1) * HEAD_W]
        return jnp.dot(k, qt_ref[b, h], preferred_element_type=F32)

    def step(j, masked):
        ahead = {}
        for c, (b, h) in enumerate(chains):
            t = c + SCORE_LOOKAHEAD
            if t < n_chains:
                ahead[t] = scores(j, t)
            elif not masked:
                s_ref[t - n_chains] = scores(j + 1, t - n_chains)
            s = s_ref[c] if c < SCORE_LOOKAHEAD else ahead.pop(c)
            if masked:
                key = lax.broadcasted_iota(jnp.int32, s.shape, 0)
                qry = lax.broadcasted_iota(jnp.int32, s.shape, 1)
                qry = jnp.where(qry >= blk, qry - blk, qry)
                s = jnp.where(key <= qry, s, NEG_INF)
            m_old = m_ref[c]
            m_new = jnp.maximum(m_old, jnp.max(s, axis=0, keepdims=True))
            alpha = jnp.exp2(m_old - m_new)
            p = jnp.exp2(s - m_new).astype(BF16)
            pv = jnp.dot(vt_ref[b, j, h], p, preferred_element_type=F32)
            acc_ref[c] = alpha * acc_ref[c] + pv
            m_ref[c] = m_new

    def group_body(g, carry):
        for u in range(KEY_BLOCK_GROUP):
            step(g * KEY_BLOCK_GROUP + u, False)
        return carry

    def single_body(j, carry):
        step(j, False)
        return carry

    for c in range(SCORE_LOOKAHEAD):
        s_ref[c] = scores(0, c)
    n_groups = i // KEY_BLOCK_GROUP
    lax.fori_loop(0, n_groups, group_body, 0)
    lax.fori_loop(n_groups * KEY_BLOCK_GROUP, i, single_body, 0)
    step(i, True)

    lam = (jnp.exp(jnp.sum(lq1_ref[...] * lk1_ref[...], axis=-1, keepdims=True))
           - jnp.exp(jnp.sum(lq2_ref[...] * lk2_ref[...], axis=-1, keepdims=True))
           + lambda_init)
    for c, (b, h) in enumerate(chains):
        acc = acc_ref[c]
        ot = acc[:HEAD_W] / acc[HEAD_W:HEAD_W + 1]
        ot = ot[:, :blk] - lam * ot[:, blk:]
        ot = ot * lax.rsqrt(jnp.mean(ot * ot, axis=0, keepdims=True) + SUBLN_EPS)
        o_ref[b, :, h * HEAD_W:(h + 1) * HEAD_W] = (
            ot.T * g_ref[...] * (1.0 - lambda_init)).astype(o_ref.dtype)


def _diff_attn(qt, k, vt, lq1, lk1, lq2, lk2, subln_g, lambda_init):
    b, t, width = k.shape
    blk = vt.shape[-1]
    nb = ATTN_BATCHES if b % ATTN_BATCHES == 0 else 1
    small = _resident((1, DA_QK))
    return pl.pallas_call(
        functools.partial(_diff_attn_kernel, blk=blk, lambda_init=lambda_init),
        grid=(b // nb, t // blk),
        in_specs=[
            pl.BlockSpec((nb, None) + qt.shape[2:], lambda bi, i: (bi, i, 0, 0, 0)),
            pl.BlockSpec((nb, t, width), lambda bi, i: (bi, 0, 0)),
            pl.BlockSpec((nb,) + vt.shape[1:], lambda bi, i: (bi, 0, 0, 0, 0)),
            small, small, small, small, _resident((1, HEAD_W)),
        ],
        out_specs=pl.BlockSpec((nb, blk, width), lambda bi, i: (bi, i, 0)),
        out_shape=jax.ShapeDtypeStruct((b, t, width), BF16),
        scratch_shapes=[
            pltpu.VMEM((nb * DA_HEADS, 1, 2 * blk), F32),
            pltpu.VMEM((nb * DA_HEADS, HEAD_W + ONES_ROWS, 2 * blk), F32),
            pltpu.VMEM((SCORE_LOOKAHEAD, blk, 2 * blk), F32),
        ],
        compiler_params=_params("parallel", "arbitrary"),
        name="diff_attn",
    )(qt, k, vt, lq1, lk1, lq2, lk2, subln_g)


def _merge_ffn_kernel(x_ref, ya_ref, yr_ref, ga_ref, gr_ref, wa_ref, wr_ref, wo_ref, mix_g_ref,
                      pre_g_ref, wg_ref, wu_ref, wd_ref, post_g_ref, o_ref):
    rows = x_ref.shape[0] // ROW_GROUPS
    groups = [slice(r * rows, (r + 1) * rows) for r in range(ROW_GROUPS)]
    xs = []
    for sl in groups:
        ya = jnp.dot(ya_ref[sl, :], wa_ref[...], preferred_element_type=F32)
        yr = jnp.dot(yr_ref[sl, :], wr_ref[...], preferred_element_type=F32)
        merged = ga_ref[sl, :].astype(F32) * ya + gr_ref[sl, :].astype(F32) * yr
        y = jnp.dot(merged.astype(BF16), wo_ref[...], preferred_element_type=F32)
        xs.append(x_ref[sl, :] + _rmsnorm(y, mix_g_ref[...], NORM_EPS))
    for sl, x in zip(groups, xs):
        o_ref[sl, :] = _ffn_rows(x, pre_g_ref, wg_ref, wu_ref, wd_ref, post_g_ref)


def _merge_ffn(x2d, ya, yr, ga, gr, wa, wr, wo, mix_g, pre_g, wg, wu, wd, post_g):
    n, d = x2d.shape
    tm = min(FFN_TOKENS, n)

    def row(w):
        return pl.BlockSpec((tm, w), lambda i: (i, 0))

    weights = (wa, wr, wo, mix_g, pre_g, wg, wu, wd, post_g)
    return pl.pallas_call(
        _merge_ffn_kernel,
        grid=(n // tm,),
        in_specs=[row(d), row(ya.shape[1]), row(yr.shape[1]), row(d), row(d)]
        + [_resident(w.shape) for w in weights],
        out_specs=row(d),
        out_shape=jax.ShapeDtypeStruct((n, d), F32),
        compiler_params=_params("parallel"),
        name="merge_ffn",
    )(x2d, ya, yr, ga, gr, *weights)


def _retention_tables(t):
    angle = 1.0 / (10000.0 ** jnp.linspace(0.0, 1.0, HEAD_W // 2, dtype=F32))
    angle = jnp.repeat(angle, 2)
    ang = jnp.arange(t, dtype=F32)[:, None] * angle[None, :]
    log_g = jnp.log(1.0 - 2.0 ** (-5.0 - jnp.arange(RET_HEADS, dtype=F32)))
    log_g = jnp.broadcast_to(log_g[:, None, None], (RET_HEADS, 1, HEAD_W))
    return jnp.sin(ang), jnp.cos(ang), log_g


def kernel(x, ffn1_pre_g, ffn1_w_gate, ffn1_w_up, ffn1_w_down, ffn1_post_g, mix_pre_g, w_in, lambda_q1, lambda_k1, lambda_q2, lambda_k2, diff_subln_g, w_attn_proj, w_ret_proj, w_out, mix_post_g, ffn2_pre_g, ffn2_w_gate, ffn2_w_up, ffn2_w_down, ffn2_post_g):
    b, t, d = x.shape
    depth = w_in.shape[0]
    n = b * t
    da_w = DA_HEADS * HEAD_W
    sin, cos, log_g = _retention_tables(t)
    bf = lambda w: w.astype(BF16)

    x2d = x.reshape(n, d)
    for l in range(depth):
        lambda_init = 0.8 - 0.6 * math.exp(-0.3 * l)
        x2d, qt, k, vt, ga, gr, yr = _mix_proj(
            x2d, ffn1_pre_g[l][None], bf(ffn1_w_gate[l]), bf(ffn1_w_up[l]), bf(ffn1_w_down[l]),
            ffn1_post_g[l][None], mix_pre_g[l][None], bf(w_in[l]), sin, cos, log_g, seq_len=t)
        ya = _diff_attn(qt.reshape((b, -1) + qt.shape[1:]), k.reshape(b, t, da_w),
                        vt.reshape((b, -1) + vt.shape[1:]), lambda_q1[l][None], lambda_k1[l][None],
                        lambda_q2[l][None], lambda_k2[l][None], diff_subln_g[l][None], lambda_init)
        x2d = _merge_ffn(x2d, ya.reshape(n, da_w), yr, ga, gr,
                         bf(w_attn_proj[l]), bf(w_ret_proj[l]), bf(w_out[l]), mix_post_g[l][None],
                         ffn2_pre_g[l][None], bf(ffn2_w_gate[l]), bf(ffn2_w_up[l]),
                         bf(ffn2_w_down[l]), ffn2_post_g[l][None])
    return x2d.reshape(b, t, d)
```

```python
import functools
import math

import jax
import jax.numpy as jnp
from jax import lax
from jax.experimental import pallas as pl
from jax.experimental.pallas import tpu as pltpu

F32 = jnp.float32
BF16 = jnp.bfloat16

DA_HEADS = 4
DA_QK = 64
HEAD_W = 128
RET_HEADS = 4
NORM_EPS = 1e-6
SUBLN_EPS = 1e-5
NEG_INF = -1e30
Q_SCALE = DA_QK ** -0.5 * math.log2(math.e)
ONES_ROWS = 16

VMEM_LIMIT_BYTES = 56 * 1024 * 1024

FFN_TOKENS = 512
ROW_GROUPS = 2
ATTN_BLOCK = 256
ATTN_BATCHES = 2
KEY_BLOCK_UNROLLS = (4, 2, 1)
SCORE_LOOKAHEAD = 2


def _params(*sem):
    return pltpu.CompilerParams(dimension_semantics=sem, vmem_limit_bytes=VMEM_LIMIT_BYTES)


def _resident(shape):
    nd = len(shape)
    return pl.BlockSpec(shape, lambda *_: (0,) * nd, pipeline_mode=pl.Buffered(1))


def _sigmoid(x):
    return 0.5 * jnp.tanh(0.5 * x) + 0.5


def _rmsnorm(x, g, eps):
    return x * lax.rsqrt(jnp.mean(x * x, axis=-1, keepdims=True) + eps) * g


def _ffn_rows(x, pre_g_ref, wg_ref, wu_ref, wd_ref, post_g_ref):
    h = _rmsnorm(x, pre_g_ref[...], NORM_EPS).astype(BF16)
    g = jnp.dot(h, wg_ref[...], preferred_element_type=F32)
    u = jnp.dot(h, wu_ref[...], preferred_element_type=F32)
    a = (g * _sigmoid(g) * u).astype(BF16)
    y = jnp.dot(a, wd_ref[...], preferred_element_type=F32)
    return x + 0.5 * _rmsnorm(y, post_g_ref[...], NORM_EPS)


def _rotate_every_two(x):
    lane = lax.broadcasted_iota(jnp.int32, x.shape, 1)
    nxt = pltpu.roll(x, HEAD_W - 1, 1)
    prv = pltpu.roll(x, 1, 1)
    return jnp.where(lane % 2 == 0, -nxt, prv)


def _retention_stage1(zr, sin, cos, states, xi_ref, zeta_ref, logg_ref, chunk):
    width = RET_HEADS * HEAD_W
    out = []
    for h in range(RET_HEADS):
        q = zr[:, h * HEAD_W:(h + 1) * HEAD_W]
        k = zr[:, width + h * HEAD_W:width + (h + 1) * HEAD_W]
        v = zr[:, 2 * width + h * HEAD_W:2 * width + (h + 1) * HEAD_W].astype(BF16)
        q = q * cos + _rotate_every_two(q) * sin
        k = k * cos + _rotate_every_two(k) * sin
        s = lax.dot_general(q.astype(BF16), k.astype(BF16), (((1,), (1,)), ((), ())),
                            preferred_element_type=F32)
        inter = jnp.dot((q * xi_ref[h]).astype(BF16), states[h].astype(BF16),
                        preferred_element_type=F32)
        kv = lax.dot_general((k * zeta_ref[h]).astype(BF16), v, (((0,), (0,)), ((), ())),
                             preferred_element_type=F32)
        states[h] = jnp.exp(logg_ref[h] * float(chunk)) * states[h] + kv
        out.append((s, inter, v))
    return out


def _retention_stage2(stage1, gate, dmask_ref):
    ys = []
    for h, (s, inter, v) in enumerate(stage1):
        intra = jnp.dot((s * dmask_ref[h]).astype(BF16), v, preferred_element_type=F32)
        y = intra + inter
        y = y * lax.rsqrt(jnp.mean(y * y, axis=-1, keepdims=True) + NORM_EPS)
        g = gate[:, h * HEAD_W:(h + 1) * HEAD_W]
        ys.append(y * (g * _sigmoid(g)))
    return jnp.concatenate(ys, axis=1)


def _mix_proj_kernel(x_ref, pre_g_ref, wg_ref, wu_ref, wd_ref, post_g_ref,
                     g_ref, w_ref, sin_ref, cos_ref, logg_ref,
                     x1_ref, qt_ref, k_ref, vt_ref, ga_ref, gr_ref, yr_ref,
                     state_ref, dmask_ref, xi_ref, zeta_ref, *, tiles_per_seq):
    t = pl.program_id(0)
    chunk = x_ref.shape[0] // ROW_GROUPS
    width = RET_HEADS * HEAD_W
    k_scale = HEAD_W ** -0.5
    d_model = x_ref.shape[1]
    att_w = 3 * DA_HEADS * HEAD_W
    wa_ref = w_ref.at[:, :att_w]
    wr_ref = w_ref.at[:, att_w:att_w + 4 * width]
    wga_ref = w_ref.at[:, att_w + 4 * width:att_w + 4 * width + d_model]
    wgr_ref = w_ref.at[:, att_w + 4 * width + d_model:]

    @pl.when(t == 0)
    def _():
        state_ref[...] = jnp.zeros(state_ref.shape, F32)
        idx = lax.broadcasted_iota(jnp.int32, (chunk, HEAD_W), 0).astype(F32)
        qi = lax.broadcasted_iota(jnp.int32, (chunk, chunk), 0)
        ki = lax.broadcasted_iota(jnp.int32, (chunk, chunk), 1)
        diff = (qi - ki).astype(F32)
        for h in range(RET_HEADS):
            log_g = logg_ref[h]
            xi_ref[h] = jnp.exp(log_g * (idx + 1.0))
            zeta_ref[h] = jnp.exp(log_g * (chunk - 1.0 - idx)) * k_scale
            dmask_ref[h] = jnp.where(
                diff >= 0, jnp.exp(log_g[:, :1] * jnp.maximum(diff, 0.0)), 0.0) * k_scale

    keep = jnp.where(t % tiles_per_seq == 0, 0.0, 1.0)
    states = [state_ref[h] * keep for h in range(RET_HEADS)]

    groups = [slice(r * chunk, (r + 1) * chunk) for r in range(ROW_GROUPS)]
    hs = []
    for sl in groups:
        x1 = _ffn_rows(x_ref[sl, :], pre_g_ref, wg_ref, wu_ref, wd_ref, post_g_ref)
        x1_ref[sl, :] = x1
        hs.append(_rmsnorm(x1, g_ref[...], NORM_EPS).astype(BF16))
    zrs = [jnp.dot(h, wr_ref[...], preferred_element_type=F32) for h in hs]

    def gate_a(r):
        ga = jnp.dot(hs[r], wga_ref[...], preferred_element_type=F32)
        ga_ref[groups[r], :] = _sigmoid(ga).astype(ga_ref.dtype)

    def gate_r(r):
        gr = jnp.dot(hs[r], wgr_ref[...], preferred_element_type=F32)
        gr_ref[groups[r], :] = _sigmoid(gr).astype(gr_ref.dtype)

    def attn_qkv(r):
        da_w = DA_HEADS * HEAD_W
        z = jnp.dot(hs[r], wa_ref[...], preferred_element_type=F32)
        k_ref[groups[r], :] = z[:, da_w:2 * da_w].astype(BF16)
        feat = lax.broadcasted_iota(jnp.int32, (HEAD_W, chunk), 0)
        for h in range(DA_HEADS):
            qt = (z[:, h * HEAD_W:(h + 1) * HEAD_W] * Q_SCALE).T
            qt_ref[r, h] = jnp.concatenate([jnp.where(feat < DA_QK, qt, 0.0),
                                            jnp.where(feat >= DA_QK, qt, 0.0)], axis=1).astype(BF16)
            vt = z[:, 2 * da_w + h * HEAD_W:2 * da_w + (h + 1) * HEAD_W].T
            vt_ref[r, h, :HEAD_W, :] = vt.astype(BF16)
            vt_ref[r, h, HEAD_W:, :] = jnp.ones((ONES_ROWS, chunk), BF16)

    for r in range(ROW_GROUPS):
        gate_a(r)
        stage1 = _retention_stage1(zrs[r], sin_ref[groups[r], :], cos_ref[groups[r], :], states,
                                   xi_ref, zeta_ref, logg_ref, chunk)
        attn_qkv(r)
        yr = _retention_stage2(stage1, zrs[r][:, 3 * width:], dmask_ref)
        yr_ref[groups[r], :] = yr.astype(yr_ref.dtype)
        gate_r(r)
    for h in range(RET_HEADS):
        state_ref[h] = states[h]


def _mix_proj(x2d, pre_g, wg, wu, wd, post_g, g, w_in, sin, cos, log_g, seq_len):
    n, d = x2d.shape
    tm = min(FFN_TOKENS, seq_len)
    chunk = tm // ROW_GROUPS
    tiles_per_seq = seq_len // tm
    ws = (pre_g, wg, wu, wd, post_g, g, w_in)

    def row(w):
        return pl.BlockSpec((tm, w), lambda i: (i, 0))

    table = pl.BlockSpec((tm, HEAD_W), lambda i: (i % tiles_per_seq, 0))
    assert chunk == min(ATTN_BLOCK, seq_len)
    rows = (DA_HEADS * HEAD_W, d, d, RET_HEADS * HEAD_W)
    qt_shape = (n // chunk, DA_HEADS, HEAD_W, 2 * chunk)
    vt_shape = (n // chunk, DA_HEADS, HEAD_W + ONES_ROWS, chunk)

    def blocks(shape):
        return pl.BlockSpec((ROW_GROUPS,) + shape[1:], lambda i: (i, 0, 0, 0))

    return pl.pallas_call(
        functools.partial(_mix_proj_kernel, tiles_per_seq=tiles_per_seq),
        grid=(n // tm,),
        in_specs=[row(d)] + [_resident(w.shape) for w in ws]
        + [table, table, _resident(log_g.shape)],
        out_specs=[row(d), blocks(qt_shape), row(rows[0]), blocks(vt_shape)] + [row(w) for w in rows[1:]],
        out_shape=[jax.ShapeDtypeStruct((n, d), F32),
                   jax.ShapeDtypeStruct(qt_shape, BF16), jax.ShapeDtypeStruct((n, rows[0]), BF16),
                   jax.ShapeDtypeStruct(vt_shape, BF16)]
        + [jax.ShapeDtypeStruct((n, w), BF16) for w in rows[1:]],
        scratch_shapes=[pltpu.VMEM((RET_HEADS, HEAD_W, HEAD_W), F32),
                        pltpu.VMEM((RET_HEADS, chunk, chunk), F32),
                        pltpu.VMEM((RET_HEADS, chunk, HEAD_W), F32),
                        pltpu.VMEM((RET_HEADS, chunk, HEAD_W), F32)],
        compiler_params=_params("arbitrary"),
        name="mix_proj",
    )(x2d, *ws, sin, cos, log_g)


def _diff_attn_kernel(qt_ref, k_ref, vt_ref, lq1_ref, lk1_ref, lq2_ref, lk2_ref, g_ref,
                      o_ref, m_ref, acc_ref, s_ref, *, blk, lambda_init):
    i = pl.program_id(1)
    chains = [(b, h) for b in range(qt_ref.shape[0]) for h in range(DA_HEADS)]
    n_chains = len(chains)

    m_ref[...] = jnp.full(m_ref.shape, NEG_INF, F32)
    acc_ref[...] = jnp.zeros(acc_ref.shape, F32)

    def scores(j, c):
        b, h = chains[c]
        start = pl.multiple_of(j * blk, blk)
        k = k_ref[b, pl.ds(start, blk), h * HEAD_W:(h + 1) * HEAD_W]
        return jnp.dot(k, qt_ref[b, h], preferred_element_type=F32)

    def step(j, masked):
        ahead = {}
        for c, (b, h) in enumerate(chains):
            t = c + SCORE_LOOKAHEAD
            if t < n_chains:
                ahead[t] = scores(j, t)
            elif not masked:
                s_ref[t - n_chains] = scores(j + 1, t - n_chains)
            s = s_ref[c] if c < SCORE_LOOKAHEAD else ahead.pop(c)
            if masked:
                key = lax.broadcasted_iota(jnp.int32, s.shape, 0)
                qry = lax.broadcasted_iota(jnp.int32, s.shape, 1)
                qry = jnp.where(qry >= blk, qry - blk, qry)
                s = jnp.where(key <= qry, s, NEG_INF)
            m_old = m_ref[c]
            m_new = jnp.maximum(m_old, jnp.max(s, axis=0, keepdims=True))
            alpha = jnp.exp2(m_old - m_new)
            p = jnp.exp2(s - m_new).astype(BF16)
            pv = jnp.dot(vt_ref[b, j, h], p, preferred_element_type=F32)
            acc_ref[c] = alpha * acc_ref[c] + pv
            m_ref[c] = m_new

    def unrolled_body(first, unroll):
        def body(g, carry):
            for u in range(unroll):
                step(first + g * unroll + u, False)
            return carry
        return body

    for c in range(SCORE_LOOKAHEAD):
        s_ref[c] = scores(0, c)
    first = 0
    for unroll in KEY_BLOCK_UNROLLS:
        trips = (i - first) // unroll
        lax.fori_loop(0, trips, unrolled_body(first, unroll), 0)
        first = first + trips * unroll
    step(i, True)

    lam = (jnp.exp(jnp.sum(lq1_ref[...] * lk1_ref[...], axis=-1, keepdims=True))
           - jnp.exp(jnp.sum(lq2_ref[...] * lk2_ref[...], axis=-1, keepdims=True))
           + lambda_init)
    for c, (b, h) in enumerate(chains):
        acc = acc_ref[c]
        ot = acc[:HEAD_W] / acc[HEAD_W:HEAD_W + 1]
        ot = ot[:, :blk] - lam * ot[:, blk:]
        ot = ot * lax.rsqrt(jnp.mean(ot * ot, axis=0, keepdims=True) + SUBLN_EPS)
        o_ref[b, :, h * HEAD_W:(h + 1) * HEAD_W] = (
            ot.T * g_ref[...] * (1.0 - lambda_init)).astype(o_ref.dtype)


def _diff_attn(qt, k, vt, lq1, lk1, lq2, lk2, subln_g, lambda_init):
    b, t, width = k.shape
    blk = vt.shape[-1]
    nb = ATTN_BATCHES if b % ATTN_BATCHES == 0 else 1
    small = _resident((1, DA_QK))
    return pl.pallas_call(
        functools.partial(_diff_attn_kernel, blk=blk, lambda_init=lambda_init),
        grid=(b // nb, t // blk),
        in_specs=[
            pl.BlockSpec((nb, None) + qt.shape[2:], lambda bi, i: (bi, i, 0, 0, 0)),
            pl.BlockSpec((nb, t, width), lambda bi, i: (bi, 0, 0)),
            pl.BlockSpec((nb,) + vt.shape[1:], lambda bi, i: (bi, 0, 0, 0, 0)),
            small, small, small, small, _resident((1, HEAD_W)),
        ],
        out_specs=pl.BlockSpec((nb, blk, width), lambda bi, i: (bi, i, 0)),
        out_shape=jax.ShapeDtypeStruct((b, t, width), BF16),
        scratch_shapes=[
            pltpu.VMEM((nb * DA_HEADS, 1, 2 * blk), F32),
            pltpu.VMEM((nb * DA_HEADS, HEAD_W + ONES_ROWS, 2 * blk), F32),
            pltpu.VMEM((SCORE_LOOKAHEAD, blk, 2 * blk), F32),
        ],
        compiler_params=_params("parallel", "arbitrary"),
        name="diff_attn",
    )(qt, k, vt, lq1, lk1, lq2, lk2, subln_g)


def _merge_ffn_kernel(x_ref, ya_ref, yr_ref, ga_ref, gr_ref, wa_ref, wr_ref, wo_ref, mix_g_ref,
                      pre_g_ref, wg_ref, wu_ref, wd_ref, post_g_ref, o_ref):
    rows = x_ref.shape[0] // ROW_GROUPS
    groups = [slice(r * rows, (r + 1) * rows) for r in range(ROW_GROUPS)]
    xs = []
    for sl in groups:
        ya = jnp.dot(ya_ref[sl, :], wa_ref[...], preferred_element_type=F32)
        yr = jnp.dot(yr_ref[sl, :], wr_ref[...], preferred_element_type=F32)
        merged = ga_ref[sl, :].astype(F32) * ya + gr_ref[sl, :].astype(F32) * yr
        y = jnp.dot(merged.astype(BF16), wo_ref[...], preferred_element_type=F32)
        xs.append(x_ref[sl, :] + _rmsnorm(y, mix_g_ref[...], NORM_EPS))
    for sl, x in zip(groups, xs):
        o_ref[sl, :] = _ffn_rows(x, pre_g_ref, wg_ref, wu_ref, wd_ref, post_g_ref)


def _merge_ffn(x2d, ya, yr, ga, gr, wa, wr, wo, mix_g, pre_g, wg, wu, wd, post_g):
    n, d = x2d.shape
    tm = min(FFN_TOKENS, n)

    def row(w):
        return pl.BlockSpec((tm, w), lambda i: (i, 0))

    weights = (wa, wr, wo, mix_g, pre_g, wg, wu, wd, post_g)
    return pl.pallas_call(
        _merge_ffn_kernel,
        grid=(n // tm,),
        in_specs=[row(d), row(ya.shape[1]), row(yr.shape[1]), row(d), row(d)]
        + [_resident(w.shape) for w in weights],
        out_specs=row(d),
        out_shape=jax.ShapeDtypeStruct((n, d), F32),
        compiler_params=_params("parallel"),
        name="merge_ffn",
    )(x2d, ya, yr, ga, gr, *weights)


def _retention_tables(t):
    angle = 1.0 / (10000.0 ** jnp.linspace(0.0, 1.0, HEAD_W // 2, dtype=F32))
    angle = jnp.repeat(angle, 2)
    ang = jnp.arange(t, dtype=F32)[:, None] * angle[None, :]
    log_g = jnp.log(1.0 - 2.0 ** (-5.0 - jnp.arange(RET_HEADS, dtype=F32)))
    log_g = jnp.broadcast_to(log_g[:, None, None], (RET_HEADS, 1, HEAD_W))
    return jnp.sin(ang), jnp.cos(ang), log_g


def kernel(x, ffn1_pre_g, ffn1_w_gate, ffn1_w_up, ffn1_w_down, ffn1_post_g, mix_pre_g, w_in, lambda_q1, lambda_k1, lambda_q2, lambda_k2, diff_subln_g, w_attn_proj, w_ret_proj, w_out, mix_post_g, ffn2_pre_g, ffn2_w_gate, ffn2_w_up, ffn2_w_down, ffn2_post_g):
    b, t, d = x.shape
    depth = w_in.shape[0]
    n = b * t
    da_w = DA_HEADS * HEAD_W
    sin, cos, log_g = _retention_tables(t)
    bf = lambda w: w.astype(BF16)

    x2d = x.reshape(n, d)
    for l in range(depth):
        lambda_init = 0.8 - 0.6 * math.exp(-0.3 * l)
        x2d, qt, k, vt, ga, gr, yr = _mix_proj(
            x2d, ffn1_pre_g[l][None], bf(ffn1_w_gate[l]), bf(ffn1_w_up[l]), bf(ffn1_w_down[l]),
            ffn1_post_g[l][None], mix_pre_g[l][None], bf(w_in[l]), sin, cos, log_g, seq_len=t)
        ya = _diff_attn(qt.reshape((b, -1) + qt.shape[1:]), k.reshape(b, t, da_w),
                        vt.reshape((b, -1) + vt.shape[1:]), lambda_q1[l][None], lambda_k1[l][None],
                        lambda_q2[l][None], lambda_k2[l][None], diff_subln_g[l][None], lambda_init)
        x2d = _merge_ffn(x2d, ya.reshape(n, da_w), yr, ga, gr,
                         bf(w_attn_proj[l]), bf(w_ret_proj[l]), bf(w_out[l]), mix_post_g[l][None],
                         ffn2_pre_g[l][None], bf(ffn2_w_gate[l]), bf(ffn2_w_up[l]),
                         bf(ffn2_w_down[l]), ffn2_post_g[l][None])
    return x2d.reshape(b, t, d)
```

```python
import functools
import math

import jax
import jax.numpy as jnp
from jax import lax
from jax.experimental import pallas as pl
from jax.experimental.pallas import tpu as pltpu

F32 = jnp.float32
BF16 = jnp.bfloat16

DA_HEADS = 4
DA_QK = 64
HEAD_W = 128
RET_HEADS = 4
NORM_EPS = 1e-6
SUBLN_EPS = 1e-5
NEG_INF = -1e30
Q_SCALE = DA_QK ** -0.5 * math.log2(math.e)
BF16_SUBLANES = 16
ONES_ROWS = BF16_SUBLANES

VMEM_LIMIT_BYTES = 60 * 1024 * 1024

FFN_TOKENS = 512
ROW_GROUPS = 2
ATTN_BLOCK = 256
ATTN_BATCHES = 2
KEY_BLOCK_UNROLLS = (4, 2, 1)
SCORE_LOOKAHEAD = 2


def _params(*sem):
    return pltpu.CompilerParams(dimension_semantics=sem, vmem_limit_bytes=VMEM_LIMIT_BYTES)


def _resident(shape):
    nd = len(shape)
    return pl.BlockSpec(shape, lambda *_: (0,) * nd, pipeline_mode=pl.Buffered(1))


def _sigmoid(x):
    return 0.5 * jnp.tanh(0.5 * x) + 0.5


def _rmsnorm(x, g, eps):
    return x * lax.rsqrt(jnp.mean(x * x, axis=-1, keepdims=True) + eps) * g


def _ffn_rows(x, pre_g_ref, wg_ref, wu_ref, wd_ref, post_g_ref):
    h = _rmsnorm(x, pre_g_ref[...], NORM_EPS).astype(BF16)
    g = jnp.dot(h, wg_ref[...], preferred_element_type=F32)
    u = jnp.dot(h, wu_ref[...], preferred_element_type=F32)
    a = (g * _sigmoid(g) * u).astype(BF16)
    y = jnp.dot(a, wd_ref[...], preferred_element_type=F32)
    return x + 0.5 * _rmsnorm(y, post_g_ref[...], NORM_EPS)


def _rotate_every_two(x):
    lane = lax.broadcasted_iota(jnp.int32, x.shape, 1)
    nxt = pltpu.roll(x, HEAD_W - 1, 1)
    prv = pltpu.roll(x, 1, 1)
    return jnp.where(lane % 2 == 0, -nxt, prv)


def _retention_stage1(zr, sin, cos, states, xi_ref, zeta_ref, logg_ref, chunk):
    width = RET_HEADS * HEAD_W
    out = []
    for h in range(RET_HEADS):
        q = zr[:, h * HEAD_W:(h + 1) * HEAD_W]
        k = zr[:, width + h * HEAD_W:width + (h + 1) * HEAD_W]
        v = zr[:, 2 * width + h * HEAD_W:2 * width + (h + 1) * HEAD_W].astype(BF16)
        q = q * cos + _rotate_every_two(q) * sin
        k = k * cos + _rotate_every_two(k) * sin
        s = lax.dot_general(q.astype(BF16), k.astype(BF16), (((1,), (1,)), ((), ())),
                            preferred_element_type=F32)
        inter = jnp.dot((q * xi_ref[h]).astype(BF16), states[h].astype(BF16),
                        preferred_element_type=F32)
        kv = lax.dot_general((k * zeta_ref[h]).astype(BF16), v, (((0,), (0,)), ((), ())),
                             preferred_element_type=F32)
        states[h] = jnp.exp(logg_ref[h] * float(chunk)) * states[h] + kv
        out.append((s, inter, v))
    return out


def _retention_stage2(stage1, gate, dmask_ref):
    ys = []
    for h, (s, inter, v) in enumerate(stage1):
        intra = jnp.dot((s * dmask_ref[h]).astype(BF16), v, preferred_element_type=F32)
        y = intra + inter
        y = y * lax.rsqrt(jnp.mean(y * y, axis=-1, keepdims=True) + NORM_EPS)
        g = gate[:, h * HEAD_W:(h + 1) * HEAD_W]
        ys.append(y * (g * _sigmoid(g)))
    return jnp.concatenate(ys, axis=1)


def _mix_proj_kernel(x_ref, pre_g_ref, wg_ref, wu_ref, wd_ref, post_g_ref,
                     g_ref, w_ref, sin_ref, cos_ref, logg_ref,
                     *rest, tiles_per_seq, n_cast):
    cast_src, rest = rest[:n_cast], rest[n_cast:]
    x1_ref, qt_ref, k_ref, vt_ref, ga_ref, gr_ref, yr_ref = rest[:7]
    cast_dst = rest[7:7 + n_cast]
    state_ref, dmask_ref, xi_ref, zeta_ref = rest[7 + n_cast:]
    t = pl.program_id(0)
    chunk = x_ref.shape[0] // ROW_GROUPS
    width = RET_HEADS * HEAD_W
    k_scale = HEAD_W ** -0.5
    d_model = x_ref.shape[1]
    att_w = 3 * DA_HEADS * HEAD_W
    wa_ref = w_ref.at[:, :att_w]
    wr_ref = w_ref.at[:, att_w:att_w + 4 * width]
    wga_ref = w_ref.at[:, att_w + 4 * width:att_w + 4 * width + d_model]
    wgr_ref = w_ref.at[:, att_w + 4 * width + d_model:]

    @pl.when(t == 0)
    def _():
        state_ref[...] = jnp.zeros(state_ref.shape, F32)
        idx = lax.broadcasted_iota(jnp.int32, (chunk, HEAD_W), 0).astype(F32)
        qi = lax.broadcasted_iota(jnp.int32, (chunk, chunk), 0)
        ki = lax.broadcasted_iota(jnp.int32, (chunk, chunk), 1)
        diff = (qi - ki).astype(F32)
        for h in range(RET_HEADS):
            log_g = logg_ref[h]
            xi_ref[h] = jnp.exp(log_g * (idx + 1.0))
            zeta_ref[h] = jnp.exp(log_g * (chunk - 1.0 - idx)) * k_scale
            dmask_ref[h] = jnp.where(
                diff >= 0, jnp.exp(log_g[:, :1] * jnp.maximum(diff, 0.0)), 0.0) * k_scale

    keep = jnp.where(t % tiles_per_seq == 0, 0.0, 1.0)
    states = [state_ref[h] * keep for h in range(RET_HEADS)]

    groups = [slice(r * chunk, (r + 1) * chunk) for r in range(ROW_GROUPS)]
    hs = []
    for sl in groups:
        x1 = _ffn_rows(x_ref[sl, :], pre_g_ref, wg_ref, wu_ref, wd_ref, post_g_ref)
        x1_ref[sl, :] = x1
        hs.append(_rmsnorm(x1, g_ref[...], NORM_EPS).astype(BF16))
    zrs = [jnp.dot(h, wr_ref[...], preferred_element_type=F32) for h in hs]

    for src, dst in zip(cast_src, cast_dst):
        dst[...] = src[...].astype(dst.dtype)

    def gate_a(r):
        ga = jnp.dot(hs[r], wga_ref[...], preferred_element_type=F32)
        ga_ref[groups[r], :] = _sigmoid(ga).astype(ga_ref.dtype)

    def gate_r(r):
        gr = jnp.dot(hs[r], wgr_ref[...], preferred_element_type=F32)
        gr_ref[groups[r], :] = _sigmoid(gr).astype(gr_ref.dtype)

    def attn_qkv(r):
        da_w = DA_HEADS * HEAD_W
        z = jnp.dot(hs[r], wa_ref[...], preferred_element_type=F32)
        k_ref[groups[r], :] = z[:, da_w:2 * da_w].astype(BF16)
        feat = lax.broadcasted_iota(jnp.int32, (HEAD_W, chunk), 0)
        for h in range(DA_HEADS):
            qt = (z[:, h * HEAD_W:(h + 1) * HEAD_W] * Q_SCALE).T
            qt_ref[r, h] = jnp.concatenate([jnp.where(feat < DA_QK, qt, 0.0),
                                            jnp.where(feat >= DA_QK, qt, 0.0)], axis=1).astype(BF16)
            vt = z[:, 2 * da_w + h * HEAD_W:2 * da_w + (h + 1) * HEAD_W].T
            vt_ref[r, h, :HEAD_W, :] = vt.astype(BF16)
            vt_ref[r, h, HEAD_W:, :] = jnp.ones((ONES_ROWS, chunk), BF16)

    for r in range(ROW_GROUPS):
        gate_a(r)
        stage1 = _retention_stage1(zrs[r], sin_ref[groups[r], :], cos_ref[groups[r], :], states,
                                   xi_ref, zeta_ref, logg_ref, chunk)
        attn_qkv(r)
        yr = _retention_stage2(stage1, zrs[r][:, 3 * width:], dmask_ref)
        yr_ref[groups[r], :] = yr.astype(yr_ref.dtype)
        gate_r(r)
    for h in range(RET_HEADS):
        state_ref[h] = states[h]


def _mix_proj(x2d, pre_g, wg, wu, wd, post_g, g, w_in, sin, cos, log_g, seq_len, to_cast):
    n, d = x2d.shape
    tm = min(FFN_TOKENS, seq_len)
    chunk = tm // ROW_GROUPS
    tiles_per_seq = seq_len // tm
    ws = (pre_g, wg, wu, wd, post_g, g, w_in)

    def row(w):
        return pl.BlockSpec((tm, w), lambda i: (i, 0))

    table = pl.BlockSpec((tm, HEAD_W), lambda i: (i % tiles_per_seq, 0))
    assert chunk == min(ATTN_BLOCK, seq_len)
    rows = (DA_HEADS * HEAD_W, d, d, RET_HEADS * HEAD_W)
    qt_shape = (n // chunk, DA_HEADS, HEAD_W, 2 * chunk)
    vt_shape = (n // chunk, DA_HEADS, HEAD_W + ONES_ROWS, chunk)

    def blocks(shape):
        return pl.BlockSpec((ROW_GROUPS,) + shape[1:], lambda i: (i, 0, 0, 0))

    def cast_block(w):
        steps = n // tm
        rows = next(r for r in range(BF16_SUBLANES, w.shape[0] + 1, BF16_SUBLANES)
                    if w.shape[0] % r == 0 and w.shape[0] // r <= steps)
        last = w.shape[0] // rows - 1
        return pl.BlockSpec((rows, w.shape[1]), lambda i: (jnp.minimum(i, last), 0))

    cast_specs = [cast_block(w) for w in to_cast]

    return pl.pallas_call(
        functools.partial(_mix_proj_kernel, tiles_per_seq=tiles_per_seq, n_cast=len(to_cast)),
        grid=(n // tm,),
        in_specs=[row(d)] + [_resident(w.shape) for w in ws]
        + [table, table, _resident(log_g.shape)] + cast_specs,
        out_specs=[row(d), blocks(qt_shape), row(rows[0]), blocks(vt_shape)] + [row(w) for w in rows[1:]]
        + cast_specs,
        out_shape=[jax.ShapeDtypeStruct((n, d), F32),
                   jax.ShapeDtypeStruct(qt_shape, BF16), jax.ShapeDtypeStruct((n, rows[0]), BF16),
                   jax.ShapeDtypeStruct(vt_shape, BF16)]
        + [jax.ShapeDtypeStruct((n, w), BF16) for w in rows[1:]]
        + [jax.ShapeDtypeStruct(w.shape, BF16) for w in to_cast],
        scratch_shapes=[pltpu.VMEM((RET_HEADS, HEAD_W, HEAD_W), F32),
                        pltpu.VMEM((RET_HEADS, chunk, chunk), F32),
                        pltpu.VMEM((RET_HEADS, chunk, HEAD_W), F32),
                        pltpu.VMEM((RET_HEADS, chunk, HEAD_W), F32)],
        compiler_params=_params("arbitrary"),
        name="mix_proj",
    )(x2d, *ws, sin, cos, log_g, *to_cast)


def _diff_attn_kernel(qt_ref, k_ref, vt_ref, lq1_ref, lk1_ref, lq2_ref, lk2_ref, g_ref,
                      o_ref, m_ref, acc_ref, s_ref, *, blk, lambda_init):
    i = pl.program_id(1)
    chains = [(b, h) for b in range(qt_ref.shape[0]) for h in range(DA_HEADS)]
    n_chains = len(chains)

    m_ref[...] = jnp.full(m_ref.shape, NEG_INF, F32)
    acc_ref[...] = jnp.zeros(acc_ref.shape, F32)

    def scores(j, c):
        b, h = chains[c]
        start = pl.multiple_of(j * blk, blk)
        k = k_ref[b, pl.ds(start, blk), h * HEAD_W:(h + 1) * HEAD_W]
        return jnp.dot(k, qt_ref[b, h], preferred_element_type=F32)

    def step(j, masked):
        ahead = {}
        for c, (b, h) in enumerate(chains):
            t = c + SCORE_LOOKAHEAD
            if t < n_chains:
                ahead[t] = scores(j, t)
            elif not masked:
                s_ref[t - n_chains] = scores(j + 1, t - n_chains)
            s = s_ref[c] if c < SCORE_LOOKAHEAD else ahead.pop(c)
            if masked:
                key = lax.broadcasted_iota(jnp.int32, s.shape, 0)
                qry = lax.broadcasted_iota(jnp.int32, s.shape, 1)
                qry = jnp.where(qry >= blk, qry - blk, qry)
                s = jnp.where(key <= qry, s, NEG_INF)
            m_old = m_ref[c]
            m_new = jnp.maximum(m_old, jnp.max(s, axis=0, keepdims=True))
            alpha = jnp.exp2(m_old - m_new)
            p = jnp.exp2(s - m_new).astype(BF16)
            pv = jnp.dot(vt_ref[b, j, h], p, preferred_element_type=F32)
            acc_ref[c] = alpha * acc_ref[c] + pv
            m_ref[c] = m_new

    def unrolled_body(first, unroll):
        def body(g, carry):
            for u in range(unroll):
                step(first + g * unroll + u, False)
            return carry
        return body

    for c in range(SCORE_LOOKAHEAD):
        s_ref[c] = scores(0, c)
    first = 0
    for unroll in KEY_BLOCK_UNROLLS:
        trips = (i - first) // unroll
        lax.fori_loop(0, trips, unrolled_body(first, unroll), 0)
        first = first + trips * unroll
    step(i, True)

    lam = (jnp.exp(jnp.sum(lq1_ref[...] * lk1_ref[...], axis=-1, keepdims=True))
           - jnp.exp(jnp.sum(lq2_ref[...] * lk2_ref[...], axis=-1, keepdims=True))
           + lambda_init)
    for c, (b, h) in enumerate(chains):
        acc = acc_ref[c]
        ot = acc[:HEAD_W] / acc[HEAD_W:HEAD_W + 1]
        ot = ot[:, :blk] - lam * ot[:, blk:]
        ot = ot * lax.rsqrt(jnp.mean(ot * ot, axis=0, keepdims=True) + SUBLN_EPS)
        o_ref[b, :, h * HEAD_W:(h + 1) * HEAD_W] = (
            ot.T * g_ref[...] * (1.0 - lambda_init)).astype(o_ref.dtype)


def _diff_attn(qt, k, vt, lq1, lk1, lq2, lk2, subln_g, lambda_init):
    b, t, width = k.shape
    blk = vt.shape[-1]
    nb = ATTN_BATCHES if b % ATTN_BATCHES == 0 else 1
    small = _resident((1, DA_QK))
    return pl.pallas_call(
        functools.partial(_diff_attn_kernel, blk=blk, lambda_init=lambda_init),
        grid=(b // nb, t // blk),
        in_specs=[
            pl.BlockSpec((nb, None) + qt.shape[2:], lambda bi, i: (bi, i, 0, 0, 0)),
            pl.BlockSpec((nb, t, width), lambda bi, i: (bi, 0, 0)),
            pl.BlockSpec((nb,) + vt.shape[1:], lambda bi, i: (bi, 0, 0, 0, 0)),
            small, small, small, small, _resident((1, HEAD_W)),
        ],
        out_specs=pl.BlockSpec((nb, blk, width), lambda bi, i: (bi, i, 0)),
        out_shape=jax.ShapeDtypeStruct((b, t, width), BF16),
        scratch_shapes=[
            pltpu.VMEM((nb * DA_HEADS, 1, 2 * blk), F32),
            pltpu.VMEM((nb * DA_HEADS, HEAD_W + ONES_ROWS, 2 * blk), F32),
            pltpu.VMEM((SCORE_LOOKAHEAD, blk, 2 * blk), F32),
        ],
        compiler_params=_params("parallel", "arbitrary"),
        name="diff_attn",
    )(qt, k, vt, lq1, lk1, lq2, lk2, subln_g)


def _merge_ffn_kernel(x_ref, ya_ref, yr_ref, ga_ref, gr_ref, wa_ref, wr_ref, wo_ref, mix_g_ref,
                      pre_g_ref, wg_ref, wu_ref, wd_ref, post_g_ref, o_ref):
    rows = x_ref.shape[0] // ROW_GROUPS
    groups = [slice(r * rows, (r + 1) * rows) for r in range(ROW_GROUPS)]
    xs = []
    for sl in groups:
        ya = jnp.dot(ya_ref[sl, :], wa_ref[...], preferred_element_type=F32)
        yr = jnp.dot(yr_ref[sl, :], wr_ref[...], preferred_element_type=F32)
        merged = ga_ref[sl, :].astype(F32) * ya + gr_ref[sl, :].astype(F32) * yr
        y = jnp.dot(merged.astype(BF16), wo_ref[...], preferred_element_type=F32)
        xs.append(x_ref[sl, :] + _rmsnorm(y, mix_g_ref[...], NORM_EPS))
    for sl, x in zip(groups, xs):
        o_ref[sl, :] = _ffn_rows(x, pre_g_ref, wg_ref, wu_ref, wd_ref, post_g_ref)


def _merge_ffn(x2d, ya, yr, ga, gr, wa, wr, wo, mix_g, pre_g, wg, wu, wd, post_g):
    n, d = x2d.shape
    tm = min(FFN_TOKENS, n)

    def row(w):
        return pl.BlockSpec((tm, w), lambda i: (i, 0))

    weights = (wa, wr, wo, mix_g, pre_g, wg, wu, wd, post_g)
    return pl.pallas_call(
        _merge_ffn_kernel,
        grid=(n // tm,),
        in_specs=[row(d), row(ya.shape[1]), row(yr.shape[1]), row(d), row(d)]
        + [_resident(w.shape) for w in weights],
        out_specs=row(d),
        out_shape=jax.ShapeDtypeStruct((n, d), F32),
        compiler_params=_params("parallel"),
        name="merge_ffn",
    )(x2d, ya, yr, ga, gr, *weights)


def _retention_tables(t):
    angle = 1.0 / (10000.0 ** jnp.linspace(0.0, 1.0, HEAD_W // 2, dtype=F32))
    angle = jnp.repeat(angle, 2)
    ang = jnp.arange(t, dtype=F32)[:, None] * angle[None, :]
    log_g = jnp.log(1.0 - 2.0 ** (-5.0 - jnp.arange(RET_HEADS, dtype=F32)))
    log_g = jnp.broadcast_to(log_g[:, None, None], (RET_HEADS, 1, HEAD_W))
    return jnp.sin(ang), jnp.cos(ang), log_g


def kernel(x, ffn1_pre_g, ffn1_w_gate, ffn1_w_up, ffn1_w_down, ffn1_post_g, mix_pre_g, w_in, lambda_q1, lambda_k1, lambda_q2, lambda_k2, diff_subln_g, w_attn_proj, w_ret_proj, w_out, mix_post_g, ffn2_pre_g, ffn2_w_gate, ffn2_w_up, ffn2_w_down, ffn2_post_g):
    b, t, d = x.shape
    depth = w_in.shape[0]
    n = b * t
    da_w = DA_HEADS * HEAD_W
    sin, cos, log_g = _retention_tables(t)
    bf = lambda w: w.astype(BF16)

    x2d = x.reshape(n, d)
    for l in range(depth):
        lambda_init = 0.8 - 0.6 * math.exp(-0.3 * l)
        late = (w_attn_proj[l], w_ret_proj[l], w_out[l], ffn2_w_gate[l], ffn2_w_up[l], ffn2_w_down[l])
        x2d, qt, k, vt, ga, gr, yr, *late = _mix_proj(
            x2d, ffn1_pre_g[l][None], bf(ffn1_w_gate[l]), bf(ffn1_w_up[l]), bf(ffn1_w_down[l]),
            ffn1_post_g[l][None], mix_pre_g[l][None], bf(w_in[l]), sin, cos, log_g, seq_len=t,
            to_cast=late)
        w_a, w_r, w_o, w_g2, w_u2, w_d2 = late
        ya = _diff_attn(qt.reshape((b, -1) + qt.shape[1:]), k.reshape(b, t, da_w),
                        vt.reshape((b, -1) + vt.shape[1:]), lambda_q1[l][None], lambda_k1[l][None],
                        lambda_q2[l][None], lambda_k2[l][None], diff_subln_g[l][None], lambda_init)
        x2d = _merge_ffn(x2d, ya.reshape(n, da_w), yr, ga, gr, w_a, w_r, w_o, mix_post_g[l][None],
                         ffn2_pre_g[l][None], w_g2, w_u2, w_d2, ffn2_post_g[l][None])
    return x2d.reshape(b, t, d)
```

```python
import functools
import math

import jax
import jax.numpy as jnp
from jax import lax
from jax.experimental import pallas as pl
from jax.experimental.pallas import tpu as pltpu

F32 = jnp.float32
BF16 = jnp.bfloat16

DA_HEADS = 4
DA_QK = 64
HEAD_W = 128
RET_HEADS = 4
NORM_EPS = 1e-6
SUBLN_EPS = 1e-5
NEG_INF = -1e30
Q_SCALE = DA_QK ** -0.5 * math.log2(math.e)
BF16_SUBLANES = 16
ONES_ROWS = BF16_SUBLANES

VMEM_LIMIT_BYTES = 60 * 1024 * 1024

FFN_TOKENS = 512
ROW_GROUPS = 2
ATTN_BLOCK = 256
ATTN_BATCHES = 2
KEY_BLOCK_UNROLLS = (4, 2, 1)
SCORE_LOOKAHEAD = 2


def _params(*sem):
    return pltpu.CompilerParams(dimension_semantics=sem, vmem_limit_bytes=VMEM_LIMIT_BYTES)


def _resident(shape):
    nd = len(shape)
    return pl.BlockSpec(shape, lambda *_: (0,) * nd, pipeline_mode=pl.Buffered(1))


def _sigmoid(x):
    return 0.5 * jnp.tanh(0.5 * x) + 0.5


def _rmsnorm(x, g, eps):
    return x * lax.rsqrt(jnp.mean(x * x, axis=-1, keepdims=True) + eps) * g


def _ffn_rows(x, pre_g_ref, wg_ref, wu_ref, wd_ref, post_g_ref):
    h = _rmsnorm(x, pre_g_ref[...], NORM_EPS).astype(BF16)
    g = jnp.dot(h, wg_ref[...], preferred_element_type=F32)
    u = jnp.dot(h, wu_ref[...], preferred_element_type=F32)
    a = (g * _sigmoid(g) * u).astype(BF16)
    y = jnp.dot(a, wd_ref[...], preferred_element_type=F32)
    return x + 0.5 * _rmsnorm(y, post_g_ref[...], NORM_EPS)


def _rotate_every_two(x):
    lane = lax.broadcasted_iota(jnp.int32, x.shape, 1)
    nxt = pltpu.roll(x, HEAD_W - 1, 1)
    prv = pltpu.roll(x, 1, 1)
    return jnp.where(lane % 2 == 0, -nxt, prv)


def _retention_stage1(zr, sin, cos, states, xi_ref, zeta_ref, logg_ref, chunk):
    width = RET_HEADS * HEAD_W
    out = []
    for h in range(RET_HEADS):
        q = zr[:, h * HEAD_W:(h + 1) * HEAD_W]
        k = zr[:, width + h * HEAD_W:width + (h + 1) * HEAD_W]
        v = zr[:, 2 * width + h * HEAD_W:2 * width + (h + 1) * HEAD_W].astype(BF16)
        q = q * cos + _rotate_every_two(q) * sin
        k = k * cos + _rotate_every_two(k) * sin
        s = lax.dot_general(q.astype(BF16), k.astype(BF16), (((1,), (1,)), ((), ())),
                            preferred_element_type=F32)
        inter = jnp.dot((q * xi_ref[h]).astype(BF16), states[h].astype(BF16),
                        preferred_element_type=F32)
        kv = lax.dot_general((k * zeta_ref[h]).astype(BF16), v, (((0,), (0,)), ((), ())),
                             preferred_element_type=F32)
        states[h] = jnp.exp(logg_ref[h] * float(chunk)) * states[h] + kv
        out.append((s, inter, v))
    return out


def _retention_stage2(stage1, gate, dmask_ref):
    ys = []
    for h, (s, inter, v) in enumerate(stage1):
        intra = jnp.dot((s * dmask_ref[h]).astype(BF16), v, preferred_element_type=F32)
        y = intra + inter
        y = y * lax.rsqrt(jnp.mean(y * y, axis=-1, keepdims=True) + NORM_EPS)
        g = gate[:, h * HEAD_W:(h + 1) * HEAD_W]
        ys.append(y * (g * _sigmoid(g)))
    return jnp.concatenate(ys, axis=1)


def _mix_proj_kernel(x_ref, pre_g_ref, wg_ref, wu_ref, wd_ref, post_g_ref,
                     g_ref, w_ref, sin_ref, cos_ref, logg_ref,
                     *rest, tiles_per_seq, n_cast):
    cast_src, rest = rest[:n_cast], rest[n_cast:]
    x1_ref, qt_ref, k_ref, vt_ref, ga_ref, gr_ref, yr_ref = rest[:7]
    cast_dst = rest[7:7 + n_cast]
    state_ref, dmask_ref, xi_ref, zeta_ref = rest[7 + n_cast:]
    t = pl.program_id(0)
    chunk = x_ref.shape[0] // ROW_GROUPS
    width = RET_HEADS * HEAD_W
    k_scale = HEAD_W ** -0.5
    d_model = x_ref.shape[1]
    att_w = 3 * DA_HEADS * HEAD_W
    wa_ref = w_ref.at[:, :att_w]
    wr_ref = w_ref.at[:, att_w:att_w + 4 * width]
    wga_ref = w_ref.at[:, att_w + 4 * width:att_w + 4 * width + d_model]
    wgr_ref = w_ref.at[:, att_w + 4 * width + d_model:]

    @pl.when(t == 0)
    def _():
        state_ref[...] = jnp.zeros(state_ref.shape, F32)
        idx = lax.broadcasted_iota(jnp.int32, (chunk, HEAD_W), 0).astype(F32)
        qi = lax.broadcasted_iota(jnp.int32, (chunk, chunk), 0)
        ki = lax.broadcasted_iota(jnp.int32, (chunk, chunk), 1)
        diff = (qi - ki).astype(F32)
        for h in range(RET_HEADS):
            log_g = logg_ref[h]
            xi_ref[h] = jnp.exp(log_g * (idx + 1.0))
            zeta_ref[h] = jnp.exp(log_g * (chunk - 1.0 - idx)) * k_scale
            dmask_ref[h] = jnp.where(
                diff >= 0, jnp.exp(log_g[:, :1] * jnp.maximum(diff, 0.0)), 0.0) * k_scale

    fresh = t % tiles_per_seq == 0
    states = [jnp.where(fresh, 0.0, state_ref[h]) for h in range(RET_HEADS)]

    groups = [slice(r * chunk, (r + 1) * chunk) for r in range(ROW_GROUPS)]
    hs = []
    for sl in groups:
        x1 = _ffn_rows(x_ref[sl, :], pre_g_ref, wg_ref, wu_ref, wd_ref, post_g_ref)
        x1_ref[sl, :] = x1
        hs.append(_rmsnorm(x1, g_ref[...], NORM_EPS).astype(BF16))
    zrs = [jnp.dot(h, wr_ref[...], preferred_element_type=F32) for h in hs]

    for src, dst in zip(cast_src, cast_dst):
        dst[...] = src[...].astype(dst.dtype)

    def gate_a(r):
        ga = jnp.dot(hs[r], wga_ref[...], preferred_element_type=F32)
        ga_ref[groups[r], :] = _sigmoid(ga).astype(ga_ref.dtype)

    def gate_r(r):
        gr = jnp.dot(hs[r], wgr_ref[...], preferred_element_type=F32)
        gr_ref[groups[r], :] = _sigmoid(gr).astype(gr_ref.dtype)

    def attn_qkv(r):
        da_w = DA_HEADS * HEAD_W
        z = jnp.dot(hs[r], wa_ref[...], preferred_element_type=F32)
        k_ref[groups[r], :] = z[:, da_w:2 * da_w].astype(BF16)
        feat = lax.broadcasted_iota(jnp.int32, (HEAD_W, chunk), 0)
        for h in range(DA_HEADS):
            qt = (z[:, h * HEAD_W:(h + 1) * HEAD_W] * Q_SCALE).T
            qt_ref[r, h] = jnp.concatenate([jnp.where(feat < DA_QK, qt, 0.0),
                                            jnp.where(feat >= DA_QK, qt, 0.0)], axis=1).astype(BF16)
            vt = z[:, 2 * da_w + h * HEAD_W:2 * da_w + (h + 1) * HEAD_W].T
            vt_ref[r, h, :HEAD_W, :] = vt.astype(BF16)
            vt_ref[r, h, HEAD_W:, :] = jnp.ones((ONES_ROWS, chunk), BF16)

    for r in range(ROW_GROUPS):
        gate_a(r)
        stage1 = _retention_stage1(zrs[r], sin_ref[groups[r], :], cos_ref[groups[r], :], states,
                                   xi_ref, zeta_ref, logg_ref, chunk)
        attn_qkv(r)
        yr = _retention_stage2(stage1, zrs[r][:, 3 * width:], dmask_ref)
        yr_ref[groups[r], :] = yr.astype(yr_ref.dtype)
        gate_r(r)
    for h in range(RET_HEADS):
        state_ref[h] = states[h]


def _mix_proj(x2d, pre_g, wg, wu, wd, post_g, g, w_in, sin, cos, log_g, seq_len, to_cast):
    n, d = x2d.shape
    tm = min(FFN_TOKENS, seq_len)
    chunk = tm // ROW_GROUPS
    tiles_per_seq = seq_len // tm
    ws = (pre_g, wg, wu, wd, post_g, g, w_in)

    def row(w):
        return pl.BlockSpec((tm, w), lambda i: (i, 0))

    table = pl.BlockSpec((tm, HEAD_W), lambda i: (i % tiles_per_seq, 0))
    assert chunk == min(ATTN_BLOCK, seq_len)
    rows = (DA_HEADS * HEAD_W, d, d, RET_HEADS * HEAD_W)
    qt_shape = (n // chunk, DA_HEADS, HEAD_W, 2 * chunk)
    vt_shape = (n // chunk, DA_HEADS, HEAD_W + ONES_ROWS, chunk)

    def blocks(shape):
        return pl.BlockSpec((ROW_GROUPS,) + shape[1:], lambda i: (i, 0, 0, 0))

    def cast_block(w):
        steps = n // tm
        rows = next(r for r in range(BF16_SUBLANES, w.shape[0] + 1, BF16_SUBLANES)
                    if w.shape[0] % r == 0 and w.shape[0] // r <= steps)
        last = w.shape[0] // rows - 1
        return pl.BlockSpec((rows, w.shape[1]), lambda i: (jnp.minimum(i, last), 0))

    cast_specs = [cast_block(w) for w in to_cast]

    return pl.pallas_call(
        functools.partial(_mix_proj_kernel, tiles_per_seq=tiles_per_seq, n_cast=len(to_cast)),
        grid=(n // tm,),
        in_specs=[row(d)] + [_resident(w.shape) for w in ws]
        + [table, table, _resident(log_g.shape)] + cast_specs,
        out_specs=[row(d), blocks(qt_shape), row(rows[0]), blocks(vt_shape)] + [row(w) for w in rows[1:]]
        + cast_specs,
        out_shape=[jax.ShapeDtypeStruct((n, d), F32),
                   jax.ShapeDtypeStruct(qt_shape, BF16), jax.ShapeDtypeStruct((n, rows[0]), BF16),
                   jax.ShapeDtypeStruct(vt_shape, BF16)]
        + [jax.ShapeDtypeStruct((n, w), BF16) for w in rows[1:]]
        + [jax.ShapeDtypeStruct(w.shape, BF16) for w in to_cast],
        scratch_shapes=[pltpu.VMEM((RET_HEADS, HEAD_W, HEAD_W), F32),
                        pltpu.VMEM((RET_HEADS, chunk, chunk), F32),
                        pltpu.VMEM((RET_HEADS, chunk, HEAD_W), F32),
                        pltpu.VMEM((RET_HEADS, chunk, HEAD_W), F32)],
        compiler_params=_params("arbitrary"),
        name="mix_proj",
    )(x2d, *ws, sin, cos, log_g, *to_cast)


def _diff_attn_kernel(qt_ref, k_ref, vt_ref, lq1_ref, lk1_ref, lq2_ref, lk2_ref, g_ref,
                      o_ref, m_ref, acc_ref, s_ref, *, blk, lambda_init):
    i = pl.program_id(1)
    chains = [(b, h) for b in range(qt_ref.shape[0]) for h in range(DA_HEADS)]
    n_chains = len(chains)

    m_ref[...] = jnp.full(m_ref.shape, NEG_INF, F32)
    acc_ref[...] = jnp.zeros(acc_ref.shape, F32)

    def scores(j, c):
        b, h = chains[c]
        start = pl.multiple_of(j * blk, blk)
        k = k_ref[b, pl.ds(start, blk), h * HEAD_W:(h + 1) * HEAD_W]
        return jnp.dot(k, qt_ref[b, h], preferred_element_type=F32)

    def step(j, masked):
        ahead = {}
        for c, (b, h) in enumerate(chains):
            t = c + SCORE_LOOKAHEAD
            if t < n_chains:
                ahead[t] = scores(j, t)
            elif not masked:
                s_ref[t - n_chains] = scores(j + 1, t - n_chains)
            s = s_ref[c] if c < SCORE_LOOKAHEAD else ahead.pop(c)
            if masked:
                key = lax.broadcasted_iota(jnp.int32, s.shape, 0)
                qry = lax.broadcasted_iota(jnp.int32, s.shape, 1)
                qry = jnp.where(qry >= blk, qry - blk, qry)
                s = jnp.where(key <= qry, s, NEG_INF)
            m_old = m_ref[c]
            m_new = jnp.maximum(m_old, jnp.max(s, axis=0, keepdims=True))
            alpha = jnp.exp2(m_old - m_new)
            p = jnp.exp2(s - m_new).astype(BF16)
            pv = jnp.dot(vt_ref[b, j, h], p, preferred_element_type=F32)
            acc_ref[c] = alpha * acc_ref[c] + pv
            m_ref[c] = m_new

    def unrolled_body(first, unroll):
        def body(g, carry):
            for u in range(unroll):
                step(first + g * unroll + u, False)
            return carry
        return body

    for c in range(SCORE_LOOKAHEAD):
        s_ref[c] = scores(0, c)
    first = 0
    for unroll in KEY_BLOCK_UNROLLS:
        trips = (i - first) // unroll
        lax.fori_loop(0, trips, unrolled_body(first, unroll), 0)
        first = first + trips * unroll
    step(i, True)

    lam = (jnp.exp(jnp.sum(lq1_ref[...] * lk1_ref[...], axis=-1, keepdims=True))
           - jnp.exp(jnp.sum(lq2_ref[...] * lk2_ref[...], axis=-1, keepdims=True))
           + lambda_init)
    for c, (b, h) in enumerate(chains):
        acc = acc_ref[c]
        ot = acc[:HEAD_W] / acc[HEAD_W:HEAD_W + 1]
        ot = ot[:, :blk] - lam * ot[:, blk:]
        ot = ot * lax.rsqrt(jnp.mean(ot * ot, axis=0, keepdims=True) + SUBLN_EPS)
        o_ref[b, :, h * HEAD_W:(h + 1) * HEAD_W] = (
            ot.T * g_ref[...] * (1.0 - lambda_init)).astype(o_ref.dtype)


def _diff_attn(qt, k, vt, lq1, lk1, lq2, lk2, subln_g, lambda_init):
    b, t, width = k.shape
    blk = vt.shape[-1]
    nb = ATTN_BATCHES if b % ATTN_BATCHES == 0 else 1
    small = _resident((1, DA_QK))
    return pl.pallas_call(
        functools.partial(_diff_attn_kernel, blk=blk, lambda_init=lambda_init),
        grid=(b // nb, t // blk),
        in_specs=[
            pl.BlockSpec((nb, None) + qt.shape[2:], lambda bi, i: (bi, i, 0, 0, 0)),
            pl.BlockSpec((nb, t, width), lambda bi, i: (bi, 0, 0)),
            pl.BlockSpec((nb,) + vt.shape[1:], lambda bi, i: (bi, 0, 0, 0, 0)),
            small, small, small, small, _resident((1, HEAD_W)),
        ],
        out_specs=pl.BlockSpec((nb, blk, width), lambda bi, i: (bi, i, 0)),
        out_shape=jax.ShapeDtypeStruct((b, t, width), BF16),
        scratch_shapes=[
            pltpu.VMEM((nb * DA_HEADS, 1, 2 * blk), F32),
            pltpu.VMEM((nb * DA_HEADS, HEAD_W + ONES_ROWS, 2 * blk), F32),
            pltpu.VMEM((SCORE_LOOKAHEAD, blk, 2 * blk), F32),
        ],
        compiler_params=_params("parallel", "arbitrary"),
        name="diff_attn",
    )(qt, k, vt, lq1, lk1, lq2, lk2, subln_g)


def _merge_ffn_kernel(x_ref, ya_ref, yr_ref, ga_ref, gr_ref, wa_ref, wr_ref, wo_ref, mix_g_ref,
                      pre_g_ref, wg_ref, wu_ref, wd_ref, post_g_ref, o_ref):
    rows = x_ref.shape[0] // ROW_GROUPS
    groups = [slice(r * rows, (r + 1) * rows) for r in range(ROW_GROUPS)]
    xs = []
    for sl in groups:
        ya = jnp.dot(ya_ref[sl, :], wa_ref[...], preferred_element_type=F32)
        yr = jnp.dot(yr_ref[sl, :], wr_ref[...], preferred_element_type=F32)
        merged = ga_ref[sl, :].astype(F32) * ya + gr_ref[sl, :].astype(F32) * yr
        y = jnp.dot(merged.astype(BF16), wo_ref[...], preferred_element_type=F32)
        xs.append(x_ref[sl, :] + _rmsnorm(y, mix_g_ref[...], NORM_EPS))
    for sl, x in zip(groups, xs):
        o_ref[sl, :] = _ffn_rows(x, pre_g_ref, wg_ref, wu_ref, wd_ref, post_g_ref)


def _merge_ffn(x2d, ya, yr, ga, gr, wa, wr, wo, mix_g, pre_g, wg, wu, wd, post_g):
    n, d = x2d.shape
    tm = min(FFN_TOKENS, n)

    def row(w):
        return pl.BlockSpec((tm, w), lambda i: (i, 0))

    weights = (wa, wr, wo, mix_g, pre_g, wg, wu, wd, post_g)
    return pl.pallas_call(
        _merge_ffn_kernel,
        grid=(n // tm,),
        in_specs=[row(d), row(ya.shape[1]), row(yr.shape[1]), row(d), row(d)]
        + [_resident(w.shape) for w in weights],
        out_specs=row(d),
        out_shape=jax.ShapeDtypeStruct((n, d), F32),
        compiler_params=_params("parallel"),
        name="merge_ffn",
    )(x2d, ya, yr, ga, gr, *weights)


def _retention_tables(t):
    angle = 1.0 / (10000.0 ** jnp.linspace(0.0, 1.0, HEAD_W // 2, dtype=F32))
    angle = jnp.repeat(angle, 2)
    ang = jnp.arange(t, dtype=F32)[:, None] * angle[None, :]
    log_g = jnp.log(1.0 - 2.0 ** (-5.0 - jnp.arange(RET_HEADS, dtype=F32)))
    log_g = jnp.broadcast_to(log_g[:, None, None], (RET_HEADS, 1, HEAD_W))
    return jnp.sin(ang), jnp.cos(ang), log_g


def kernel(x, ffn1_pre_g, ffn1_w_gate, ffn1_w_up, ffn1_w_down, ffn1_post_g, mix_pre_g, w_in, lambda_q1, lambda_k1, lambda_q2, lambda_k2, diff_subln_g, w_attn_proj, w_ret_proj, w_out, mix_post_g, ffn2_pre_g, ffn2_w_gate, ffn2_w_up, ffn2_w_down, ffn2_post_g):
    b, t, d = x.shape
    depth = w_in.shape[0]
    n = b * t
    da_w = DA_HEADS * HEAD_W
    sin, cos, log_g = _retention_tables(t)
    bf = lambda w: w.astype(BF16)

    x2d = x.reshape(n, d)
    for l in range(depth):
        lambda_init = 0.8 - 0.6 * math.exp(-0.3 * l)
        late = (w_attn_proj[l], w_ret_proj[l], w_out[l], ffn2_w_gate[l], ffn2_w_up[l], ffn2_w_down[l])
        x2d, qt, k, vt, ga, gr, yr, *late = _mix_proj(
            x2d, ffn1_pre_g[l][None], bf(ffn1_w_gate[l]), bf(ffn1_w_up[l]), bf(ffn1_w_down[l]),
            ffn1_post_g[l][None], mix_pre_g[l][None], bf(w_in[l]), sin, cos, log_g, seq_len=t,
            to_cast=late)
        w_a, w_r, w_o, w_g2, w_u2, w_d2 = late
        ya = _diff_attn(qt.reshape((b, -1) + qt.shape[1:]), k.reshape(b, t, da_w),
                        vt.reshape((b, -1) + vt.shape[1:]), lambda_q1[l][None], lambda_k1[l][None],
                        lambda_q2[l][None], lambda_k2[l][None], diff_subln_g[l][None], lambda_init)
        x2d = _merge_ffn(x2d, ya.reshape(n, da_w), yr, ga, gr, w_a, w_r, w_o, mix_post_g[l][None],
                         ffn2_pre_g[l][None], w_g2, w_u2, w_d2, ffn2_post_g[l][None])
    return x2d.reshape(b, t, d)
```

```python
import functools
import math

import jax
import jax.numpy as jnp
from jax import lax
from jax.experimental import pallas as pl
from jax.experimental.pallas import tpu as pltpu

F32 = jnp.float32
BF16 = jnp.bfloat16

DA_HEADS = 4
DA_QK = 64
HEAD_W = 128
RET_HEADS = 4
NORM_EPS = 1e-6
SUBLN_EPS = 1e-5
NEG_INF = -1e30
Q_SCALE = DA_QK ** -0.5 * math.log2(math.e)
BF16_SUBLANES = 16
ONES_ROWS = BF16_SUBLANES

VMEM_LIMIT_BYTES = 60 * 1024 * 1024

FFN_TOKENS = 512
ROW_GROUPS = 2
ATTN_BLOCK = 256
ATTN_BATCHES = 2
KEY_BLOCK_UNROLLS = (4, 2, 1)
SCORE_LOOKAHEAD = 2


def _params(*sem):
    return pltpu.CompilerParams(dimension_semantics=sem, vmem_limit_bytes=VMEM_LIMIT_BYTES)


def _resident(shape):
    nd = len(shape)
    return pl.BlockSpec(shape, lambda *_: (0,) * nd, pipeline_mode=pl.Buffered(1))


def _sigmoid(x):
    return 0.5 * jnp.tanh(0.5 * x) + 0.5


def _rmsnorm(x, g, eps):
    return x * lax.rsqrt(jnp.mean(x * x, axis=-1, keepdims=True) + eps) * g


def _ffn_raw(x, pre_g_ref, wg_ref, wu_ref, wd_ref):
    h = _rmsnorm(x, pre_g_ref[...], NORM_EPS).astype(BF16)
    g = jnp.dot(h, wg_ref[...], preferred_element_type=F32)
    u = jnp.dot(h, wu_ref[...], preferred_element_type=F32)
    a = (g * _sigmoid(g) * u).astype(BF16)
    return jnp.dot(a, wd_ref[...], preferred_element_type=F32)


def _ffn_rows(x, pre_g_ref, wg_ref, wu_ref, wd_ref, post_g_ref):
    y = _ffn_raw(x, pre_g_ref, wg_ref, wu_ref, wd_ref)
    return x + 0.5 * _rmsnorm(y, post_g_ref[...], NORM_EPS)


def _rotate_every_two(x):
    lane = lax.broadcasted_iota(jnp.int32, x.shape, 1)
    nxt = pltpu.roll(x, HEAD_W - 1, 1)
    prv = pltpu.roll(x, 1, 1)
    return jnp.where(lane % 2 == 0, -nxt, prv)


def _retention_stage1(zr, sin, cos, states, xi_ref, zeta_ref, logg_ref, chunk):
    width = RET_HEADS * HEAD_W
    out = []
    for h in range(RET_HEADS):
        q = zr[:, h * HEAD_W:(h + 1) * HEAD_W]
        k = zr[:, width + h * HEAD_W:width + (h + 1) * HEAD_W]
        v = zr[:, 2 * width + h * HEAD_W:2 * width + (h + 1) * HEAD_W].astype(BF16)
        q = q * cos + _rotate_every_two(q) * sin
        k = k * cos + _rotate_every_two(k) * sin
        s = lax.dot_general(q.astype(BF16), k.astype(BF16), (((1,), (1,)), ((), ())),
                            preferred_element_type=F32)
        inter = jnp.dot((q * xi_ref[h]).astype(BF16), states[h].astype(BF16),
                        preferred_element_type=F32)
        kv = lax.dot_general((k * zeta_ref[h]).astype(BF16), v, (((0,), (0,)), ((), ())),
                             preferred_element_type=F32)
        states[h] = jnp.exp(logg_ref[h] * float(chunk)) * states[h] + kv
        out.append((s, inter, v))
    return out


def _retention_stage2(stage1, gate, dmask_ref):
    ys = []
    for h, (s, inter, v) in enumerate(stage1):
        intra = jnp.dot((s * dmask_ref[h]).astype(BF16), v, preferred_element_type=F32)
        y = intra + inter
        y = y * lax.rsqrt(jnp.mean(y * y, axis=-1, keepdims=True) + NORM_EPS)
        g = gate[:, h * HEAD_W:(h + 1) * HEAD_W]
        ys.append(y * (g * _sigmoid(g)))
    return jnp.concatenate(ys, axis=1)


def _mix_proj_kernel(x_ref, pre_g_ref, wg_ref, wu_ref, wd_ref, post_g_ref,
                     g_ref, w_ref, sin_ref, cos_ref, logg_ref,
                     *rest, tiles_per_seq, n_cast):
    cast_src, rest = rest[:n_cast], rest[n_cast:]
    x1_ref, qt_ref, k_ref, vt_ref, ga_ref, gr_ref, yr_ref = rest[:7]
    cast_dst = rest[7:7 + n_cast]
    state_ref, dmask_ref, xi_ref, zeta_ref = rest[7 + n_cast:]
    t = pl.program_id(0)
    chunk = x_ref.shape[0] // ROW_GROUPS
    width = RET_HEADS * HEAD_W
    k_scale = HEAD_W ** -0.5
    d_model = x_ref.shape[1]
    att_w = 3 * DA_HEADS * HEAD_W
    wa_ref = w_ref.at[:, :att_w]
    wr_ref = w_ref.at[:, att_w:att_w + 4 * width]
    wga_ref = w_ref.at[:, att_w + 4 * width:att_w + 4 * width + d_model]
    wgr_ref = w_ref.at[:, att_w + 4 * width + d_model:]

    @pl.when(t == 0)
    def _():
        state_ref[...] = jnp.zeros(state_ref.shape, F32)
        idx = lax.broadcasted_iota(jnp.int32, (chunk, HEAD_W), 0).astype(F32)
        qi = lax.broadcasted_iota(jnp.int32, (chunk, chunk), 0)
        ki = lax.broadcasted_iota(jnp.int32, (chunk, chunk), 1)
        diff = (qi - ki).astype(F32)
        for h in range(RET_HEADS):
            log_g = logg_ref[h]
            xi_ref[h] = jnp.exp(log_g * (idx + 1.0))
            zeta_ref[h] = jnp.exp(log_g * (chunk - 1.0 - idx)) * k_scale
            dmask_ref[h] = jnp.where(
                diff >= 0, jnp.exp(log_g[:, :1] * jnp.maximum(diff, 0.0)), 0.0) * k_scale

    fresh = t % tiles_per_seq == 0
    states = [jnp.where(fresh, 0.0, state_ref[h]) for h in range(RET_HEADS)]

    groups = [slice(r * chunk, (r + 1) * chunk) for r in range(ROW_GROUPS)]
    hs = []
    for sl in groups:
        x1 = _ffn_rows(x_ref[sl, :], pre_g_ref, wg_ref, wu_ref, wd_ref, post_g_ref)
        x1_ref[sl, :] = x1
        hs.append(_rmsnorm(x1, g_ref[...], NORM_EPS).astype(BF16))
    zrs = [jnp.dot(h, wr_ref[...], preferred_element_type=F32) for h in hs]

    for src, dst in zip(cast_src, cast_dst):
        dst[...] = src[...].astype(dst.dtype)

    def gate_a(r):
        ga = jnp.dot(hs[r], wga_ref[...], preferred_element_type=F32)
        ga_ref[groups[r], :] = _sigmoid(ga).astype(ga_ref.dtype)

    def gate_r(r):
        gr = jnp.dot(hs[r], wgr_ref[...], preferred_element_type=F32)
        gr_ref[groups[r], :] = _sigmoid(gr).astype(gr_ref.dtype)

    def attn_qkv(r):
        da_w = DA_HEADS * HEAD_W
        z = jnp.dot(hs[r], wa_ref[...], preferred_element_type=F32)
        k_ref[groups[r], :] = z[:, da_w:2 * da_w].astype(BF16)
        feat = lax.broadcasted_iota(jnp.int32, (HEAD_W, chunk), 0)
        for h in range(DA_HEADS):
            qt = (z[:, h * HEAD_W:(h + 1) * HEAD_W] * Q_SCALE).T
            qt_ref[r, h] = jnp.concatenate([jnp.where(feat < DA_QK, qt, 0.0),
                                            jnp.where(feat >= DA_QK, qt, 0.0)], axis=1).astype(BF16)
            vt = z[:, 2 * da_w + h * HEAD_W:2 * da_w + (h + 1) * HEAD_W].T
            vt_ref[r, h, :HEAD_W, :] = vt.astype(BF16)
            vt_ref[r, h, HEAD_W:, :] = jnp.ones((ONES_ROWS, chunk), BF16)

    for r in range(ROW_GROUPS):
        gate_a(r)
        stage1 = _retention_stage1(zrs[r], sin_ref[groups[r], :], cos_ref[groups[r], :], states,
                                   xi_ref, zeta_ref, logg_ref, chunk)
        attn_qkv(r)
        yr = _retention_stage2(stage1, zrs[r][:, 3 * width:], dmask_ref)
        yr_ref[groups[r], :] = yr.astype(yr_ref.dtype)
        gate_r(r)
    for h in range(RET_HEADS):
        state_ref[h] = states[h]


def _mix_proj(x2d, pre_g, wg, wu, wd, post_g, g, w_in, sin, cos, log_g, seq_len, to_cast):
    n, d = x2d.shape
    tm = min(FFN_TOKENS, seq_len)
    chunk = tm // ROW_GROUPS
    tiles_per_seq = seq_len // tm
    ws = (pre_g, wg, wu, wd, post_g, g, w_in)

    def row(w):
        return pl.BlockSpec((tm, w), lambda i: (i, 0))

    table = pl.BlockSpec((tm, HEAD_W), lambda i: (i % tiles_per_seq, 0))
    assert chunk == min(ATTN_BLOCK, seq_len)
    rows = (DA_HEADS * HEAD_W, d, d, RET_HEADS * HEAD_W)
    qt_shape = (n // chunk, DA_HEADS, HEAD_W, 2 * chunk)
    vt_shape = (n // chunk, DA_HEADS, HEAD_W + ONES_ROWS, chunk)

    def blocks(shape):
        return pl.BlockSpec((ROW_GROUPS,) + shape[1:], lambda i: (i, 0, 0, 0))

    def cast_block(w):
        steps = n // tm
        rows = next(r for r in range(BF16_SUBLANES, w.shape[0] + 1, BF16_SUBLANES)
                    if w.shape[0] % r == 0 and w.shape[0] // r <= steps)
        last = w.shape[0] // rows - 1
        return pl.BlockSpec((rows, w.shape[1]), lambda i: (jnp.minimum(i, last), 0))

    cast_specs = [cast_block(w) for w in to_cast]

    return pl.pallas_call(
        functools.partial(_mix_proj_kernel, tiles_per_seq=tiles_per_seq, n_cast=len(to_cast)),
        grid=(n // tm,),
        in_specs=[row(d)] + [_resident(w.shape) for w in ws]
        + [table, table, _resident(log_g.shape)] + cast_specs,
        out_specs=[row(d), blocks(qt_shape), row(rows[0]), blocks(vt_shape)] + [row(w) for w in rows[1:]]
        + cast_specs,
        out_shape=[jax.ShapeDtypeStruct((n, d), F32),
                   jax.ShapeDtypeStruct(qt_shape, BF16), jax.ShapeDtypeStruct((n, rows[0]), BF16),
                   jax.ShapeDtypeStruct(vt_shape, BF16)]
        + [jax.ShapeDtypeStruct((n, w), BF16) for w in rows[1:]]
        + [jax.ShapeDtypeStruct(w.shape, BF16) for w in to_cast],
        scratch_shapes=[pltpu.VMEM((RET_HEADS, HEAD_W, HEAD_W), F32),
                        pltpu.VMEM((RET_HEADS, chunk, chunk), F32),
                        pltpu.VMEM((RET_HEADS, chunk, HEAD_W), F32),
                        pltpu.VMEM((RET_HEADS, chunk, HEAD_W), F32)],
        compiler_params=_params("arbitrary"),
        name="mix_proj",
    )(x2d, *ws, sin, cos, log_g, *to_cast)


def _diff_attn_kernel(qt_ref, k_ref, vt_ref, lq1_ref, lk1_ref, lq2_ref, lk2_ref, g_ref,
                      o_ref, m_ref, acc_ref, s_ref, *, blk, lambda_init):
    i = pl.program_id(1)
    chains = [(b, h) for b in range(qt_ref.shape[0]) for h in range(DA_HEADS)]
    n_chains = len(chains)

    m_ref[...] = jnp.full(m_ref.shape, NEG_INF, F32)
    acc_ref[...] = jnp.zeros(acc_ref.shape, F32)

    def scores(j, c):
        b, h = chains[c]
        start = pl.multiple_of(j * blk, blk)
        k = k_ref[b, pl.ds(start, blk), h * HEAD_W:(h + 1) * HEAD_W]
        return jnp.dot(k, qt_ref[b, h], preferred_element_type=F32)

    def step(j, masked):
        ahead = {}
        for c, (b, h) in enumerate(chains):
            t = c + SCORE_LOOKAHEAD
            if t < n_chains:
                ahead[t] = scores(j, t)
            elif not masked:
                s_ref[t - n_chains] = scores(j + 1, t - n_chains)
            s = s_ref[c] if c < SCORE_LOOKAHEAD else ahead.pop(c)
            if masked:
                key = lax.broadcasted_iota(jnp.int32, s.shape, 0)
                qry = lax.broadcasted_iota(jnp.int32, s.shape, 1)
                qry = jnp.where(qry >= blk, qry - blk, qry)
                s = jnp.where(key <= qry, s, NEG_INF)
            m_old = m_ref[c]
            m_new = jnp.maximum(m_old, jnp.max(s, axis=0, keepdims=True))
            alpha = jnp.exp2(m_old - m_new)
            p = jnp.exp2(s - m_new).astype(BF16)
            pv = jnp.dot(vt_ref[b, j, h], p, preferred_element_type=F32)
            acc_ref[c] = alpha * acc_ref[c] + pv
            m_ref[c] = m_new

    def unrolled_body(first, unroll):
        def body(g, carry):
            for u in range(unroll):
                step(first + g * unroll + u, False)
            return carry
        return body

    for c in range(SCORE_LOOKAHEAD):
        s_ref[c] = scores(0, c)
    first = 0
    for unroll in KEY_BLOCK_UNROLLS:
        trips = (i - first) // unroll
        lax.fori_loop(0, trips, unrolled_body(first, unroll), 0)
        first = first + trips * unroll
    step(i, True)

    lam = (jnp.exp(jnp.sum(lq1_ref[...] * lk1_ref[...], axis=-1, keepdims=True))
           - jnp.exp(jnp.sum(lq2_ref[...] * lk2_ref[...], axis=-1, keepdims=True))
           + lambda_init)
    for c, (b, h) in enumerate(chains):
        acc = acc_ref[c]
        ot = acc[:HEAD_W] / acc[HEAD_W:HEAD_W + 1]
        ot = ot[:, :blk] - lam * ot[:, blk:]
        ot = ot * lax.rsqrt(jnp.mean(ot * ot, axis=0, keepdims=True) + SUBLN_EPS)
        o_ref[b, :, h * HEAD_W:(h + 1) * HEAD_W] = (
            ot.T * g_ref[...] * (1.0 - lambda_init)).astype(o_ref.dtype)


def _diff_attn(qt, k, vt, lq1, lk1, lq2, lk2, subln_g, lambda_init):
    b, t, width = k.shape
    blk = vt.shape[-1]
    nb = ATTN_BATCHES if b % ATTN_BATCHES == 0 else 1
    small = _resident((1, DA_QK))
    return pl.pallas_call(
        functools.partial(_diff_attn_kernel, blk=blk, lambda_init=lambda_init),
        grid=(b // nb, t // blk),
        in_specs=[
            pl.BlockSpec((nb, None) + qt.shape[2:], lambda bi, i: (bi, i, 0, 0, 0)),
            pl.BlockSpec((nb, t, width), lambda bi, i: (bi, 0, 0)),
            pl.BlockSpec((nb,) + vt.shape[1:], lambda bi, i: (bi, 0, 0, 0, 0)),
            small, small, small, small, _resident((1, HEAD_W)),
        ],
        out_specs=pl.BlockSpec((nb, blk, width), lambda bi, i: (bi, i, 0)),
        out_shape=jax.ShapeDtypeStruct((b, t, width), BF16),
        scratch_shapes=[
            pltpu.VMEM((nb * DA_HEADS, 1, 2 * blk), F32),
            pltpu.VMEM((nb * DA_HEADS, HEAD_W + ONES_ROWS, 2 * blk), F32),
            pltpu.VMEM((SCORE_LOOKAHEAD, blk, 2 * blk), F32),
        ],
        compiler_params=_params("parallel", "arbitrary"),
        name="diff_attn",
    )(qt, k, vt, lq1, lk1, lq2, lk2, subln_g)


def _merge_ffn_kernel(x_ref, ya_ref, yr_ref, ga_ref, gr_ref, wa_ref, wr_ref, wo_ref, mix_g_ref,
                      pre_g_ref, wg_ref, wu_ref, wd_ref, post_g_ref, o_ref,
                      done_ref, x_pend_ref, y_pend_ref, *, n_tiles):
    t = pl.program_id(0)
    rows = x_ref.shape[0] // ROW_GROUPS
    groups = [slice(r * rows, (r + 1) * rows) for r in range(ROW_GROUPS)]
    early, last = groups[:-1], groups[-1]

    def emit_previous():
        for sl in early:
            o_ref[sl, :] = done_ref[sl, :]
        o_ref[last, :] = x_pend_ref[...] + 0.5 * _rmsnorm(y_pend_ref[...], post_g_ref[...], NORM_EPS)

    def compute_current():
        xs = []
        for sl in groups:
            ya = jnp.dot(ya_ref[sl, :], wa_ref[...], preferred_element_type=F32)
            yr = jnp.dot(yr_ref[sl, :], wr_ref[...], preferred_element_type=F32)
            merged = ga_ref[sl, :].astype(F32) * ya + gr_ref[sl, :].astype(F32) * yr
            y = jnp.dot(merged.astype(BF16), wo_ref[...], preferred_element_type=F32)
            xs.append(x_ref[sl, :] + _rmsnorm(y, mix_g_ref[...], NORM_EPS))
        for sl, x in zip(early, xs):
            done_ref[sl, :] = _ffn_rows(x, pre_g_ref, wg_ref, wu_ref, wd_ref, post_g_ref)
        x_pend_ref[...] = xs[-1]
        y_pend_ref[...] = _ffn_raw(xs[-1], pre_g_ref, wg_ref, wu_ref, wd_ref)

    @pl.when(t == 0)
    def _():
        compute_current()

    @pl.when(jnp.logical_and(t > 0, t < n_tiles))
    def _():
        emit_previous()
        compute_current()

    @pl.when(t == n_tiles)
    def _():
        emit_previous()


def _merge_ffn(x2d, ya, yr, ga, gr, wa, wr, wo, mix_g, pre_g, wg, wu, wd, post_g):
    n, d = x2d.shape
    tm = min(FFN_TOKENS, n)
    n_tiles = n // tm

    def row(w):
        return pl.BlockSpec((tm, w), lambda i: (jnp.minimum(i, n_tiles - 1), 0))

    weights = (wa, wr, wo, mix_g, pre_g, wg, wu, wd, post_g)
    return pl.pallas_call(
        functools.partial(_merge_ffn_kernel, n_tiles=n_tiles),
        grid=(n_tiles + 1,),
        in_specs=[row(d), row(ya.shape[1]), row(yr.shape[1]), row(d), row(d)]
        + [_resident(w.shape) for w in weights],
        out_specs=pl.BlockSpec((tm, d), lambda i: (jnp.maximum(i - 1, 0), 0)),
        out_shape=jax.ShapeDtypeStruct((n, d), F32),
        scratch_shapes=[pltpu.VMEM((tm, d), F32),
                        pltpu.VMEM((tm // ROW_GROUPS, d), F32),
                        pltpu.VMEM((tm // ROW_GROUPS, d), F32)],
        compiler_params=_params("arbitrary"),
        name="merge_ffn",
    )(x2d, ya, yr, ga, gr, *weights)


def _retention_tables(t):
    angle = 1.0 / (10000.0 ** jnp.linspace(0.0, 1.0, HEAD_W // 2, dtype=F32))
    angle = jnp.repeat(angle, 2)
    ang = jnp.arange(t, dtype=F32)[:, None] * angle[None, :]
    log_g = jnp.log(1.0 - 2.0 ** (-5.0 - jnp.arange(RET_HEADS, dtype=F32)))
    log_g = jnp.broadcast_to(log_g[:, None, None], (RET_HEADS, 1, HEAD_W))
    return jnp.sin(ang), jnp.cos(ang), log_g


def kernel(x, ffn1_pre_g, ffn1_w_gate, ffn1_w_up, ffn1_w_down, ffn1_post_g, mix_pre_g, w_in, lambda_q1, lambda_k1, lambda_q2, lambda_k2, diff_subln_g, w_attn_proj, w_ret_proj, w_out, mix_post_g, ffn2_pre_g, ffn2_w_gate, ffn2_w_up, ffn2_w_down, ffn2_post_g):
    b, t, d = x.shape
    depth = w_in.shape[0]
    n = b * t
    da_w = DA_HEADS * HEAD_W
    sin, cos, log_g = _retention_tables(t)
    bf = lambda w: w.astype(BF16)

    x2d = x.reshape(n, d)
    for l in range(depth):
        lambda_init = 0.8 - 0.6 * math.exp(-0.3 * l)
        late = (w_attn_proj[l], w_ret_proj[l], w_out[l], ffn2_w_gate[l], ffn2_w_up[l], ffn2_w_down[l])
        x2d, qt, k, vt, ga, gr, yr, *late = _mix_proj(
            x2d, ffn1_pre_g[l][None], bf(ffn1_w_gate[l]), bf(ffn1_w_up[l]), bf(ffn1_w_down[l]),
            ffn1_post_g[l][None], mix_pre_g[l][None], bf(w_in[l]), sin, cos, log_g, seq_len=t,
            to_cast=late)
        w_a, w_r, w_o, w_g2, w_u2, w_d2 = late
        ya = _diff_attn(qt.reshape((b, -1) + qt.shape[1:]), k.reshape(b, t, da_w),
                        vt.reshape((b, -1) + vt.shape[1:]), lambda_q1[l][None], lambda_k1[l][None],
                        lambda_q2[l][None], lambda_k2[l][None], diff_subln_g[l][None], lambda_init)
        x2d = _merge_ffn(x2d, ya.reshape(n, da_w), yr, ga, gr, w_a, w_r, w_o, mix_post_g[l][None],
                         ffn2_pre_g[l][None], w_g2, w_u2, w_d2, ffn2_post_g[l][None])
    return x2d.reshape(b, t, d)
```

```python
import functools
import math

import jax
import jax.numpy as jnp
from jax import lax
from jax.experimental import pallas as pl
from jax.experimental.pallas import tpu as pltpu

F32 = jnp.float32
BF16 = jnp.bfloat16

DA_HEADS = 4
DA_QK = 64
HEAD_W = 128
RET_HEADS = 4
NORM_EPS = 1e-6
SUBLN_EPS = 1e-5
NEG_INF = -1e30
Q_SCALE = DA_QK ** -0.5 * math.log2(math.e)
BF16_SUBLANES = 16
ONES_ROWS = BF16_SUBLANES

VMEM_LIMIT_BYTES = 60 * 1024 * 1024

FFN_TOKENS = 512
ROW_GROUPS = 2
ATTN_BLOCK = 256
ATTN_BATCHES = 2
KEY_BLOCK_UNROLLS = (8, 4, 2, 1)
SCORE_LOOKAHEAD = 2


def _params(*sem):
    return pltpu.CompilerParams(dimension_semantics=sem, vmem_limit_bytes=VMEM_LIMIT_BYTES)


def _resident(shape):
    nd = len(shape)
    return pl.BlockSpec(shape, lambda *_: (0,) * nd, pipeline_mode=pl.Buffered(1))


def _sigmoid(x):
    return 0.5 * jnp.tanh(0.5 * x) + 0.5


def _rmsnorm(x, g, eps):
    return x * lax.rsqrt(jnp.mean(x * x, axis=-1, keepdims=True) + eps) * g


def _ffn_rows(x, pre_g_ref, wg_ref, wu_ref, wd_ref, post_g_ref):
    h = _rmsnorm(x, pre_g_ref[...], NORM_EPS).astype(BF16)
    g = jnp.dot(h, wg_ref[...], preferred_element_type=F32)
    u = jnp.dot(h, wu_ref[...], preferred_element_type=F32)
    a = (g * _sigmoid(g) * u).astype(BF16)
    y = jnp.dot(a, wd_ref[...], preferred_element_type=F32)
    return x + 0.5 * _rmsnorm(y, post_g_ref[...], NORM_EPS)


def _rotate_every_two(x):
    lane = lax.broadcasted_iota(jnp.int32, x.shape, 1)
    nxt = pltpu.roll(x, HEAD_W - 1, 1)
    prv = pltpu.roll(x, 1, 1)
    return jnp.where(lane % 2 == 0, -nxt, prv)


def _retention_stage1(zr, sin, cos, states, xi_ref, zeta_ref, logg_ref, chunk):
    width = RET_HEADS * HEAD_W
    out = []
    for h in range(RET_HEADS):
        q = zr[:, h * HEAD_W:(h + 1) * HEAD_W]
        k = zr[:, width + h * HEAD_W:width + (h + 1) * HEAD_W]
        v = zr[:, 2 * width + h * HEAD_W:2 * width + (h + 1) * HEAD_W].astype(BF16)
        q = q * cos + _rotate_every_two(q) * sin
        k = k * cos + _rotate_every_two(k) * sin
        s = lax.dot_general(q.astype(BF16), k.astype(BF16), (((1,), (1,)), ((), ())),
                            preferred_element_type=F32)
        inter = jnp.dot((q * xi_ref[h]).astype(BF16), states[h].astype(BF16),
                        preferred_element_type=F32)
        kv = lax.dot_general((k * zeta_ref[h]).astype(BF16), v, (((0,), (0,)), ((), ())),
                             preferred_element_type=F32)
        states[h] = jnp.exp(logg_ref[h] * float(chunk)) * states[h] + kv
        out.append((s, inter, v))
    return out


def _retention_stage2(stage1, gate, dmask_ref):
    ys = []
    for h, (s, inter, v) in enumerate(stage1):
        intra = jnp.dot((s * dmask_ref[h]).astype(BF16), v, preferred_element_type=F32)
        y = intra + inter
        y = y * lax.rsqrt(jnp.mean(y * y, axis=-1, keepdims=True) + NORM_EPS)
        g = gate[:, h * HEAD_W:(h + 1) * HEAD_W]
        ys.append(y * (g * _sigmoid(g)))
    return jnp.concatenate(ys, axis=1)


def _mix_proj_kernel(x_ref, pre_g_ref, wg_ref, wu_ref, wd_ref, post_g_ref,
                     g_ref, w_ref, sin_ref, cos_ref, logg_ref,
                     *rest, tiles_per_seq, n_cast):
    cast_src, rest = rest[:n_cast], rest[n_cast:]
    x1_ref, qt_ref, k_ref, vt_ref, ga_ref, gr_ref, yr_ref = rest[:7]
    cast_dst = rest[7:7 + n_cast]
    state_ref, dmask_ref, xi_ref, zeta_ref = rest[7 + n_cast:]
    t = pl.program_id(0)
    chunk = x_ref.shape[0] // ROW_GROUPS
    width = RET_HEADS * HEAD_W
    k_scale = HEAD_W ** -0.5
    d_model = x_ref.shape[1]
    att_w = 3 * DA_HEADS * HEAD_W
    wa_ref = w_ref.at[:, :att_w]
    wr_ref = w_ref.at[:, att_w:att_w + 4 * width]
    wga_ref = w_ref.at[:, att_w + 4 * width:att_w + 4 * width + d_model]
    wgr_ref = w_ref.at[:, att_w + 4 * width + d_model:]

    @pl.when(t == 0)
    def _():
        state_ref[...] = jnp.zeros(state_ref.shape, F32)
        idx = lax.broadcasted_iota(jnp.int32, (chunk, HEAD_W), 0).astype(F32)
        qi = lax.broadcasted_iota(jnp.int32, (chunk, chunk), 0)
        ki = lax.broadcasted_iota(jnp.int32, (chunk, chunk), 1)
        diff = (qi - ki).astype(F32)
        for h in range(RET_HEADS):
            log_g = logg_ref[h]
            xi_ref[h] = jnp.exp(log_g * (idx + 1.0))
            zeta_ref[h] = jnp.exp(log_g * (chunk - 1.0 - idx)) * k_scale
            dmask_ref[h] = jnp.where(
                diff >= 0, jnp.exp(log_g[:, :1] * jnp.maximum(diff, 0.0)), 0.0) * k_scale

    fresh = t % tiles_per_seq == 0
    states = [jnp.where(fresh, 0.0, state_ref[h]) for h in range(RET_HEADS)]

    groups = [slice(r * chunk, (r + 1) * chunk) for r in range(ROW_GROUPS)]
    hs = []
    for sl in groups:
        x1 = _ffn_rows(x_ref[sl, :], pre_g_ref, wg_ref, wu_ref, wd_ref, post_g_ref)
        x1_ref[sl, :] = x1
        hs.append(_rmsnorm(x1, g_ref[...], NORM_EPS).astype(BF16))
    zrs = [jnp.dot(h, wr_ref[...], preferred_element_type=F32) for h in hs]

    for src, dst in zip(cast_src, cast_dst):
        dst[...] = src[...].astype(dst.dtype)

    def gate_a(r):
        ga = jnp.dot(hs[r], wga_ref[...], preferred_element_type=F32)
        ga_ref[groups[r], :] = _sigmoid(ga).astype(ga_ref.dtype)

    def gate_r(r):
        gr = jnp.dot(hs[r], wgr_ref[...], preferred_element_type=F32)
        gr_ref[groups[r], :] = _sigmoid(gr).astype(gr_ref.dtype)

    def attn_qkv(r):
        da_w = DA_HEADS * HEAD_W
        z = jnp.dot(hs[r], wa_ref[...], preferred_element_type=F32)
        k_ref[groups[r], :] = z[:, da_w:2 * da_w].astype(BF16)
        feat = lax.broadcasted_iota(jnp.int32, (HEAD_W, chunk), 0)
        for h in range(DA_HEADS):
            qt = (z[:, h * HEAD_W:(h + 1) * HEAD_W] * Q_SCALE).T
            qt_ref[r, h] = jnp.concatenate([jnp.where(feat < DA_QK, qt, 0.0),
                                            jnp.where(feat >= DA_QK, qt, 0.0)], axis=1).astype(BF16)
            vt = z[:, 2 * da_w + h * HEAD_W:2 * da_w + (h + 1) * HEAD_W].T
            vt_ref[r, h, :HEAD_W, :] = vt.astype(BF16)
            vt_ref[r, h, HEAD_W:, :] = jnp.ones((ONES_ROWS, chunk), BF16)

    for r in range(ROW_GROUPS):
        gate_a(r)
        stage1 = _retention_stage1(zrs[r], sin_ref[groups[r], :], cos_ref[groups[r], :], states,
                                   xi_ref, zeta_ref, logg_ref, chunk)
        attn_qkv(r)
        yr = _retention_stage2(stage1, zrs[r][:, 3 * width:], dmask_ref)
        yr_ref[groups[r], :] = yr.astype(yr_ref.dtype)
        gate_r(r)
    for h in range(RET_HEADS):
        state_ref[h] = states[h]


def _mix_proj(x2d, pre_g, wg, wu, wd, post_g, g, w_in, sin, cos, log_g, seq_len, to_cast):
    n, d = x2d.shape
    tm = min(FFN_TOKENS, seq_len)
    chunk = tm // ROW_GROUPS
    tiles_per_seq = seq_len // tm
    ws = (pre_g, wg, wu, wd, post_g, g, w_in)

    def row(w):
        return pl.BlockSpec((tm, w), lambda i: (i, 0))

    table = pl.BlockSpec((tm, HEAD_W), lambda i: (i % tiles_per_seq, 0))
    assert chunk == min(ATTN_BLOCK, seq_len)
    rows = (DA_HEADS * HEAD_W, d, d, RET_HEADS * HEAD_W)
    qt_shape = (n // chunk, DA_HEADS, HEAD_W, 2 * chunk)
    vt_shape = (n // chunk, DA_HEADS, HEAD_W + ONES_ROWS, chunk)

    def blocks(shape):
        return pl.BlockSpec((ROW_GROUPS,) + shape[1:], lambda i: (i, 0, 0, 0))

    def cast_block(w):
        steps = n // tm
        rows = next(r for r in range(BF16_SUBLANES, w.shape[0] + 1, BF16_SUBLANES)
                    if w.shape[0] % r == 0 and w.shape[0] // r <= steps)
        last = w.shape[0] // rows - 1
        return pl.BlockSpec((rows, w.shape[1]), lambda i: (jnp.minimum(i, last), 0))

    cast_specs = [cast_block(w) for w in to_cast]

    return pl.pallas_call(
        functools.partial(_mix_proj_kernel, tiles_per_seq=tiles_per_seq, n_cast=len(to_cast)),
        grid=(n // tm,),
        in_specs=[row(d)] + [_resident(w.shape) for w in ws]
        + [table, table, _resident(log_g.shape)] + cast_specs,
        out_specs=[row(d), blocks(qt_shape), row(rows[0]), blocks(vt_shape)] + [row(w) for w in rows[1:]]
        + cast_specs,
        out_shape=[jax.ShapeDtypeStruct((n, d), F32),
                   jax.ShapeDtypeStruct(qt_shape, BF16), jax.ShapeDtypeStruct((n, rows[0]), BF16),
                   jax.ShapeDtypeStruct(vt_shape, BF16)]
        + [jax.ShapeDtypeStruct((n, w), BF16) for w in rows[1:]]
        + [jax.ShapeDtypeStruct(w.shape, BF16) for w in to_cast],
        scratch_shapes=[pltpu.VMEM((RET_HEADS, HEAD_W, HEAD_W), F32),
                        pltpu.VMEM((RET_HEADS, chunk, chunk), F32),
                        pltpu.VMEM((RET_HEADS, chunk, HEAD_W), F32),
                        pltpu.VMEM((RET_HEADS, chunk, HEAD_W), F32)],
        compiler_params=_params("arbitrary"),
        name="mix_proj",
    )(x2d, *ws, sin, cos, log_g, *to_cast)


def _diff_attn_kernel(qt_ref, k_ref, vt_ref, lq1_ref, lk1_ref, lq2_ref, lk2_ref, g_ref,
                      o_ref, m_ref, acc_ref, s_ref, *, blk, lambda_init):
    i = pl.program_id(1)
    chains = [(b, h) for b in range(qt_ref.shape[0]) for h in range(DA_HEADS)]
    n_chains = len(chains)

    m_ref[...] = jnp.full(m_ref.shape, NEG_INF, F32)
    acc_ref[...] = jnp.zeros(acc_ref.shape, F32)

    def scores(j, c):
        b, h = chains[c]
        start = pl.multiple_of(j * blk, blk)
        k = k_ref[b, pl.ds(start, blk), h * HEAD_W:(h + 1) * HEAD_W]
        return jnp.dot(k, qt_ref[b, h], preferred_element_type=F32)

    def step(j, masked):
        ahead = {}
        for c, (b, h) in enumerate(chains):
            t = c + SCORE_LOOKAHEAD
            if t < n_chains:
                ahead[t] = scores(j, t)
            elif not masked:
                s_ref[t - n_chains] = scores(j + 1, t - n_chains)
            s = s_ref[c] if c < SCORE_LOOKAHEAD else ahead.pop(c)
            if masked:
                key = lax.broadcasted_iota(jnp.int32, s.shape, 0)
                qry = lax.broadcasted_iota(jnp.int32, s.shape, 1)
                qry = jnp.where(qry >= blk, qry - blk, qry)
                s = jnp.where(key <= qry, s, NEG_INF)
            m_old = m_ref[c]
            m_new = jnp.maximum(m_old, jnp.max(s, axis=0, keepdims=True))
            alpha = jnp.exp2(m_old - m_new)
            p = jnp.exp2(s - m_new).astype(BF16)
            pv = jnp.dot(vt_ref[b, j, h], p, preferred_element_type=F32)
            acc_ref[c] = alpha * acc_ref[c] + pv
            m_ref[c] = m_new

    def unrolled_body(first, unroll):
        def body(g, carry):
            for u in range(unroll):
                step(first + g * unroll + u, False)
            return carry
        return body

    for c in range(SCORE_LOOKAHEAD):
        s_ref[c] = scores(0, c)
    first = 0
    for unroll in KEY_BLOCK_UNROLLS:
        trips = (i - first) // unroll
        lax.fori_loop(0, trips, unrolled_body(first, unroll), 0)
        first = first + trips * unroll
    step(i, True)

    lam = (jnp.exp(jnp.sum(lq1_ref[...] * lk1_ref[...], axis=-1, keepdims=True))
           - jnp.exp(jnp.sum(lq2_ref[...] * lk2_ref[...], axis=-1, keepdims=True))
           + lambda_init)
    for c, (b, h) in enumerate(chains):
        acc = acc_ref[c]
        ot = acc[:HEAD_W] / acc[HEAD_W:HEAD_W + 1]
        ot = ot[:, :blk] - lam * ot[:, blk:]
        ot = ot * lax.rsqrt(jnp.mean(ot * ot, axis=0, keepdims=True) + SUBLN_EPS)
        o_ref[b, :, h * HEAD_W:(h + 1) * HEAD_W] = (
            ot.T * g_ref[...] * (1.0 - lambda_init)).astype(o_ref.dtype)


def _diff_attn(qt, k, vt, lq1, lk1, lq2, lk2, subln_g, lambda_init):
    b, t, width = k.shape
    blk = vt.shape[-1]
    nb = ATTN_BATCHES if b % ATTN_BATCHES == 0 else 1
    small = _resident((1, DA_QK))
    return pl.pallas_call(
        functools.partial(_diff_attn_kernel, blk=blk, lambda_init=lambda_init),
        grid=(b // nb, t // blk),
        in_specs=[
            pl.BlockSpec((nb, None) + qt.shape[2:], lambda bi, i: (bi, i, 0, 0, 0)),
            pl.BlockSpec((nb, t, width), lambda bi, i: (bi, 0, 0)),
            pl.BlockSpec((nb,) + vt.shape[1:], lambda bi, i: (bi, 0, 0, 0, 0)),
            small, small, small, small, _resident((1, HEAD_W)),
        ],
        out_specs=pl.BlockSpec((nb, blk, width), lambda bi, i: (bi, i, 0)),
        out_shape=jax.ShapeDtypeStruct((b, t, width), BF16),
        scratch_shapes=[
            pltpu.VMEM((nb * DA_HEADS, 1, 2 * blk), F32),
            pltpu.VMEM((nb * DA_HEADS, HEAD_W + ONES_ROWS, 2 * blk), F32),
            pltpu.VMEM((SCORE_LOOKAHEAD, blk, 2 * blk), F32),
        ],
        compiler_params=_params("parallel", "arbitrary"),
        name="diff_attn",
    )(qt, k, vt, lq1, lk1, lq2, lk2, subln_g)


def _merge_ffn_kernel(x_ref, ya_ref, yr_ref, ga_ref, gr_ref, wa_ref, wr_ref, wo_ref, mix_g_ref,
                      pre_g_ref, wg_ref, wu_ref, wd_ref, post_g_ref, o_ref):
    rows = x_ref.shape[0] // ROW_GROUPS
    groups = [slice(r * rows, (r + 1) * rows) for r in range(ROW_GROUPS)]
    xs = []
    for sl in groups:
        ya = jnp.dot(ya_ref[sl, :], wa_ref[...], preferred_element_type=F32)
        yr = jnp.dot(yr_ref[sl, :], wr_ref[...], preferred_element_type=F32)
        merged = ga_ref[sl, :].astype(F32) * ya + gr_ref[sl, :].astype(F32) * yr
        y = jnp.dot(merged.astype(BF16), wo_ref[...], preferred_element_type=F32)
        xs.append(x_ref[sl, :] + _rmsnorm(y, mix_g_ref[...], NORM_EPS))
    for sl, x in zip(groups, xs):
        o_ref[sl, :] = _ffn_rows(x, pre_g_ref, wg_ref, wu_ref, wd_ref, post_g_ref)


def _merge_ffn(x2d, ya, yr, ga, gr, wa, wr, wo, mix_g, pre_g, wg, wu, wd, post_g):
    n, d = x2d.shape
    tm = min(FFN_TOKENS, n)

    def row(w):
        return pl.BlockSpec((tm, w), lambda i: (i, 0))

    weights = (wa, wr, wo, mix_g, pre_g, wg, wu, wd, post_g)
    return pl.pallas_call(
        _merge_ffn_kernel,
        grid=(n // tm,),
        in_specs=[row(d), row(ya.shape[1]), row(yr.shape[1]), row(d), row(d)]
        + [_resident(w.shape) for w in weights],
        out_specs=row(d),
        out_shape=jax.ShapeDtypeStruct((n, d), F32),
        compiler_params=_params("parallel"),
        name="merge_ffn",
    )(x2d, ya, yr, ga, gr, *weights)


def _retention_tables(t):
    angle = 1.0 / (10000.0 ** jnp.linspace(0.0, 1.0, HEAD_W // 2, dtype=F32))
    angle = jnp.repeat(angle, 2)
    ang = jnp.arange(t, dtype=F32)[:, None] * angle[None, :]
    log_g = jnp.log(1.0 - 2.0 ** (-5.0 - jnp.arange(RET_HEADS, dtype=F32)))
    log_g = jnp.broadcast_to(log_g[:, None, None], (RET_HEADS, 1, HEAD_W))
    return jnp.sin(ang), jnp.cos(ang), log_g


def kernel(x, ffn1_pre_g, ffn1_w_gate, ffn1_w_up, ffn1_w_down, ffn1_post_g, mix_pre_g, w_in, lambda_q1, lambda_k1, lambda_q2, lambda_k2, diff_subln_g, w_attn_proj, w_ret_proj, w_out, mix_post_g, ffn2_pre_g, ffn2_w_gate, ffn2_w_up, ffn2_w_down, ffn2_post_g):
    b, t, d = x.shape
    depth = w_in.shape[0]
    n = b * t
    da_w = DA_HEADS * HEAD_W
    sin, cos, log_g = _retention_tables(t)
    bf = lambda w: w.astype(BF16)

    x2d = x.reshape(n, d)
    for l in range(depth):
        lambda_init = 0.8 - 0.6 * math.exp(-0.3 * l)
        late = (w_attn_proj[l], w_ret_proj[l], w_out[l], ffn2_w_gate[l], ffn2_w_up[l], ffn2_w_down[l])
        x2d, qt, k, vt, ga, gr, yr, *late = _mix_proj(
            x2d, ffn1_pre_g[l][None], bf(ffn1_w_gate[l]), bf(ffn1_w_up[l]), bf(ffn1_w_down[l]),
            ffn1_post_g[l][None], mix_pre_g[l][None], bf(w_in[l]), sin, cos, log_g, seq_len=t,
            to_cast=late)
        w_a, w_r, w_o, w_g2, w_u2, w_d2 = late
        ya = _diff_attn(qt.reshape((b, -1) + qt.shape[1:]), k.reshape(b, t, da_w),
                        vt.reshape((b, -1) + vt.shape[1:]), lambda_q1[l][None], lambda_k1[l][None],
                        lambda_q2[l][None], lambda_k2[l][None], diff_subln_g[l][None], lambda_init)
        x2d = _merge_ffn(x2d, ya.reshape(n, da_w), yr, ga, gr, w_a, w_r, w_o, mix_post_g[l][None],
                         ffn2_pre_g[l][None], w_g2, w_u2, w_d2, ffn2_post_g[l][None])
    return x2d.reshape(b, t, d)
```
